```python
import jax
import jax.numpy as jnp
from jax import lax
import numpy as np

D_MODEL = 1024
BATCH = 8
SEQ = 4096
DEPTH = 2

HEAD_DIM = 64
ATT_WIDTH = 3 * D_MODEL // 8
RET_WIDTH = 3 * D_MODEL // 8
HGRN_WIDTH = D_MODEL - ATT_WIDTH - RET_WIDTH
ATT_HEADS = ATT_WIDTH // HEAD_DIM
RET_HEADS = RET_WIDTH // HEAD_DIM
HGRN_HEADS = HGRN_WIDTH // HEAD_DIM
MIX_WIDTH = ATT_WIDTH + RET_WIDTH + HGRN_WIDTH
IN_SPLITS = (ATT_WIDTH,) * 3 + (RET_WIDTH,) * 4 + (HGRN_WIDTH,) * 5
IN_WIDTH = sum(IN_SPLITS)
DILATED_CONFIGS = ((128, 1), (512, 4), (2048, 16))
RET_CHUNK = 128
HGRN_CHUNK = 16
FFN_HIDDEN = ((-(-8 * D_MODEL // 3)) + 255) // 256 * 256
N_MOD = 6
NORM_EPS = 1e-6
NEG_INF = -1e30
LB_FLOOR = 1e-30

kernel_name = 'hybrid_dilated_retention_hgrn2_encoder'


def rms_norm(x, gain):
    xf = x.astype(jnp.float32)
    y = xf * lax.rsqrt(jnp.mean(xf * xf, axis=-1, keepdims=True) + NORM_EPS)
    return (y * gain.astype(jnp.float32)).astype(x.dtype)


def dilated_branch(q, k, v, slopes, window, dilation):
    B, S, H, dh = q.shape
    r = dilation
    n_side = window // 2 // r
    L = S // r
    nb = -(-L // n_side)
    Lp = nb * n_side

    def to_sub(t):
        t = t.reshape(B, L, r, H, dh).transpose(0, 2, 3, 1, 4)
        return jnp.pad(t, ((0, 0), (0, 0), (0, 0), (0, Lp - L), (0, 0)))

    def key_blocks(t):
        t = jnp.pad(to_sub(t), ((0, 0), (0, 0), (0, 0), (n_side, n_side), (0, 0)))
        t = t.reshape(B, r, H, nb + 2, n_side, dh)
        return jnp.concatenate([t[:, :, :, :-2], t[:, :, :, 1:-1], t[:, :, :, 2:]], axis=4)

    qs = to_sub(q).reshape(B, r, H, nb, n_side, dh) * HEAD_DIM ** -0.5
    ks, vs = key_blocks(k), key_blocks(v)
    qi = jnp.arange(Lp).reshape(nb, n_side, 1)
    kj = jnp.arange(nb)[:, None, None] * n_side + jnp.arange(3 * n_side)[None, None, :] - n_side
    steps = jnp.abs(qi - kj)
    valid = (steps <= n_side) & (kj >= 0) & (kj < L)
    bias = -slopes[:, None, None, None] * (dilation * steps).astype(jnp.float32)
    s = jnp.einsum('brhnqd,brhnkd->brhnqk', qs, ks, preferred_element_type=jnp.float32) + bias
    s = jnp.where(valid, s, NEG_INF)
    m = jnp.max(s, axis=-1, keepdims=True)
    p = jnp.exp(s - m)
    l = jnp.sum(p, axis=-1, keepdims=True)
    o = jnp.einsum('brhnqk,brhnkd->brhnqd', p, vs.astype(jnp.float32)) / l
    lse = (m + jnp.log(l))[..., 0]
    o = o.reshape(B, r, H, Lp, dh)[:, :, :, :L].transpose(0, 3, 1, 2, 4).reshape(B, S, H, dh)
    lse = lse.reshape(B, r, H, Lp)[..., :L].transpose(0, 3, 1, 2).reshape(B, S, H)
    return o, lse


def retention_bidir(q, k, v):
    B, S, H, d = q.shape
    C = RET_CHUNK
    N = S // C

    def chunks(t):
        return t.astype(jnp.float32).reshape(B, N, C, H, d).transpose(0, 3, 1, 2, 4)

    q, k, v = chunks(q), chunks(k) * d ** -0.5, chunks(v)
    log_g = jnp.log1p(-jnp.exp2(-5.0 - jnp.arange(H, dtype=jnp.float32)))[:, None, None, None]
    pos = jnp.arange(C, dtype=jnp.float32)[:, None]
    intra = jnp.exp(log_g * jnp.abs(pos - pos.T))
    scores = jnp.einsum('bhnid,bhnjd->bhnij', q, k) * intra
    out = jnp.einsum('bhnij,bhnjd->bhnid', scores, v)
    u_fwd = jnp.einsum('bhncd,bhnce->nbhde', k * jnp.exp(log_g * (C - 1 - pos)), v)
    u_bwd = jnp.einsum('bhncd,bhnce->nbhde', k * jnp.exp(log_g * pos), v)
    chunk_decay = jnp.exp(log_g[:, :, 0] * C)

    def step(state, u):
        return chunk_decay * state + u, state

    zero = jnp.zeros((B, H, d, d), jnp.float32)
    _, r_fwd = lax.scan(step, zero, u_fwd)
    _, r_bwd = lax.scan(step, zero, u_bwd, reverse=True)
    out = (out
           + jnp.einsum('bhncd,nbhde->bhnce', q * jnp.exp(log_g * (pos + 1.0)), r_fwd)
           + jnp.einsum('bhncd,nbhde->bhnce', q * jnp.exp(log_g * (C - pos)), r_bwd))
    return out.transpose(0, 2, 3, 1, 4).reshape(B, S, H, d)


def gated_recurrence_chunked(q, log_f, k, v):
    B, S, H, dk = q.shape
    dv = v.shape[-1]
    C = HGRN_CHUNK
    N = S // C

    def chunks(t):
        return t.reshape(B, N, C, H, t.shape[-1]).transpose(0, 3, 1, 2, 4)

    q, log_f, k, v = chunks(q), chunks(log_f), chunks(k), chunks(v)
    b = jnp.cumsum(log_f, axis=3)
    tri = jnp.tril(jnp.ones((C, C), dtype=bool))[:, :, None]
    rel = jnp.where(tri, jnp.exp(jnp.minimum(b[:, :, :, :, None, :] - b[:, :, :, None, :, :], 0.0)), 0.0)
    attn = jnp.einsum('bhntd,bhnsd,bhntsd->bhnts', q, k, rel)
    out = jnp.einsum('bhnts,bhnse->bhnte', attn, v)
    b_last = b[:, :, :, -1:, :]
    u = jnp.einsum('bhnsd,bhnse->nbhde', k * jnp.exp(b_last - b), v)
    a = jnp.exp(b_last[:, :, :, 0, :]).transpose(2, 0, 1, 3)[..., None]

    def step(state, xs):
        a_n, u_n = xs
        return a_n * state + u_n, state

    _, s_prev = lax.scan(step, jnp.zeros((B, H, dk, dv), jnp.float32), (a, u))
    out = out + jnp.einsum('bhntd,nbhde->bhnte', q * jnp.exp(b), s_prev)
    return out.transpose(0, 2, 3, 1, 4).reshape(B, S, H, dv)


def hgrn2_bidir(hq, hz_fwd, hz_bwd, hi, lb):
    B, S, _ = hq.shape
    shp = (B, S, HGRN_HEADS, HEAD_DIM)
    log_lb = jnp.log(jnp.maximum(lb, LB_FLOOR))
    log_one_minus_lb = jnp.log1p(-lb)

    def gates(z):
        z = z.astype(jnp.float32)
        log_f = jnp.logaddexp(log_lb, log_one_minus_lb + jax.nn.log_sigmoid(z))
        k = (1.0 - lb) * jax.nn.sigmoid(-z)
        return log_f.reshape(shp), k.reshape(shp)

    q = jax.nn.silu(hq.astype(jnp.float32)).reshape(shp)
    v = hi.astype(jnp.float32).reshape(shp)
    lf_f, k_f = gates(hz_fwd)
    lf_b, k_b = gates(hz_bwd)
    o_f = gated_recurrence_chunked(q, lf_f, k_f, v)
    o_b = jnp.flip(gated_recurrence_chunked(jnp.flip(q, 1), jnp.flip(lf_b, 1), jnp.flip(k_b, 1), jnp.flip(v, 1)), 1)
    return (o_f + o_b).reshape(B, S, HGRN_WIDTH)


def hybrid_mixer(h, w_in, w_out, ret_gn, hgrn_gn, lb):
    B, S, _ = h.shape
    proj = h @ w_in
    cuts = [int(i) for i in np.cumsum(IN_SPLITS)[:-1]]
    aq, ak, av, rq, rk, rv, rg, hq, hzf, hzb, hi, hg = jnp.split(proj, cuts, axis=-1)

    def heads(t, n):
        return t.reshape(B, S, n, HEAD_DIM)

    q, k, v = heads(aq, ATT_HEADS), heads(ak, ATT_HEADS), heads(av, ATT_HEADS)
    slopes = jnp.exp2(-8.0 * jnp.arange(1, ATT_HEADS + 1, dtype=jnp.float32) / ATT_HEADS)
    outs, lses = [], []
    for window, dilation in DILATED_CONFIGS:
        o, l = dilated_branch(q, k, v, slopes, window, dilation)
        outs.append(o)
        lses.append(l)
    wts = jax.nn.softmax(jnp.stack(lses), axis=0)[..., None]
    att = jnp.sum(wts * jnp.stack(outs), axis=0).reshape(B, S, ATT_WIDTH)

    ret = retention_bidir(heads(rq, RET_HEADS), heads(rk, RET_HEADS), heads(rv, RET_HEADS))
    mu = jnp.mean(ret, axis=-1, keepdims=True)
    var = jnp.mean(jnp.square(ret - mu), axis=-1, keepdims=True)
    ret = ((ret - mu) * lax.rsqrt(var + NORM_EPS)).reshape(B, S, RET_WIDTH)
    ret = ret * ret_gn.astype(jnp.float32) * jax.nn.silu(rg.astype(jnp.float32))

    hgo = rms_norm(hgrn2_bidir(hq, hzf, hzb, hi, lb), hgrn_gn) * jax.nn.silu(hg.astype(jnp.float32))

    mixed = jnp.concatenate([att, ret, hgo], axis=-1).astype(h.dtype)
    return mixed @ w_out


def swiglu(h, w_gate_up, w_down):
    g, u = jnp.split(h @ w_gate_up, 2, axis=-1)
    return (jax.nn.silu(g) * u) @ w_down


def setup_inputs(seed: int = 0) -> dict:
    key = jax.random.key(seed)
    ks = jax.random.split(key, 16)

    def nrm(k, shape, scale):
        return jax.random.normal(k, shape, jnp.float32) * scale

    D = D_MODEL
    return {
        'x': nrm(ks[0], (BATCH, SEQ, D), 1.0),
        'c': nrm(ks[1], (BATCH, D), 1.0),
        'w_ada': nrm(ks[2], (DEPTH, D, N_MOD * D), 0.5 * D ** -0.5),
        'b_ada': nrm(ks[3], (DEPTH, N_MOD * D), 0.02),
        'g_mix': 1.0 + nrm(ks[4], (DEPTH, D), 0.02),
        'w_in': nrm(ks[5], (DEPTH, D, IN_WIDTH), D ** -0.5),
        'ret_gn': 1.0 + nrm(ks[6], (DEPTH, RET_WIDTH), 0.02),
        'hgrn_gn': 1.0 + nrm(ks[7], (DEPTH, HGRN_WIDTH), 0.02),
        'hgrn_lb_logits': nrm(ks[8], (DEPTH, HGRN_WIDTH), 0.5),
        'w_out': nrm(ks[9], (DEPTH, MIX_WIDTH, D), MIX_WIDTH ** -0.5),
        'g_ffn': 1.0 + nrm(ks[10], (DEPTH, D), 0.02),
        'w_gate_up': nrm(ks[11], (DEPTH, D, 2 * FFN_HIDDEN), D ** -0.5),
        'w_down': nrm(ks[12], (DEPTH, FFN_HIDDEN, D), FFN_HIDDEN ** -0.5),
        'g_final': 1.0 + nrm(ks[13], (D,), 0.02),
    }


def reference(x, c, w_ada, b_ada, g_mix, w_in, ret_gn, hgrn_gn, hgrn_lb_logits, w_out,
              g_ffn, w_gate_up, w_down, g_final):
    B, S, D = x.shape
    lb_p = jax.nn.softmax(hgrn_lb_logits.astype(jnp.float32), axis=0)
    lb_all = jnp.clip(jnp.cumsum(lb_p, axis=0) - lb_p[0:1], 0.0, 1.0 - 1e-6)
    cond = jax.nn.silu(c)
    for layer in range(DEPTH):
        mod = (cond @ w_ada[layer] + b_ada[layer]).reshape(B, N_MOD, 1, D)
        shift1, scale1, gate1 = mod[:, 0], mod[:, 1], mod[:, 2]
        shift2, scale2, gate2 = mod[:, 3], mod[:, 4], mod[:, 5]
        h = rms_norm(x, g_mix[layer]) * (1.0 + scale1) + shift1
        y = hybrid_mixer(h, w_in[layer], w_out[layer], ret_gn[layer], hgrn_gn[layer], lb_all[layer])
        x = x + (gate1 * y).astype(x.dtype)
        h = rms_norm(x, g_ffn[layer]) * (1.0 + scale2) + shift2
        x = x + (gate2 * swiglu(h, w_gate_up[layer], w_down[layer])).astype(x.dtype)
    return rms_norm(x, g_final)
```

```python
import functools

import jax
import jax.numpy as jnp
from jax import lax
from jax.experimental import pallas as pl
from jax.experimental.pallas import tpu as pltpu

D_MODEL = 1024
DEPTH = 2
HEAD_DIM = 64
ATT_WIDTH = 384
RET_WIDTH = 384
HGRN_WIDTH = 256
IN_WIDTH = 3 * ATT_WIDTH + 4 * RET_WIDTH + 5 * HGRN_WIDTH
DILATED_CONFIGS = ((128, 1), (512, 4), (2048, 16))
N_SIDE = 64
RET_CHUNK = 128
HGRN_CHUNK = 16
FFN_HIDDEN = 2816
N_MOD = 6
NORM_EPS = 1e-6
NEG_INF = -1e30
LB_FLOOR = 1e-30

LANES = 128
ATT_Q_BLK, ATT_K_BLK, ATT_V_BLK = 0, 3, 6
RET_Q_BLK, RET_K_BLK, RET_V_BLK, RET_G_BLK = 9, 12, 15, 18
HG_Q_BLK, HG_ZF_BLK, HG_ZB_BLK, HG_I_BLK, HG_G_BLK = 21, 23, 25, 27, 29

VMEM_LIMIT = 56 * 1024 * 1024

F32 = jnp.float32
BF16 = jnp.bfloat16

NT_DIMS = (((1,), (1,)), ((), ()))
TN_DIMS = (((0,), (0,)), ((), ()))


def _dot(a, b, dims=None):
    if dims is None:
        return jnp.dot(a, b, preferred_element_type=F32)
    return lax.dot_general(a, b, dims, preferred_element_type=F32)


def _split2(x):
    hi = x.astype(BF16)
    lo = (x - hi.astype(F32)).astype(BF16)
    return hi, lo


def _split3(x):
    hi = x.astype(BF16)
    r1 = x - hi.astype(F32)
    mid = r1.astype(BF16)
    lo = (r1 - mid.astype(F32)).astype(BF16)
    return hi, mid, lo


def _head_block_mask(shape):
    r = lax.broadcasted_iota(jnp.int32, shape, 0)
    c = lax.broadcasted_iota(jnp.int32, shape, 1)
    return (r >= HEAD_DIM) == (c >= HEAD_DIM)


def _head_sum_matrix():
    return jnp.where(_head_block_mask((LANES, LANES)), 1.0, 0.0).astype(BF16)


def _sigmoid(x):
    return 1.0 / (1.0 + jnp.exp(-x))


def _mod_kernel(c_ref, w_ref, b_ref, o_ref):
    c = c_ref[...]
    cond = c * _sigmoid(c)
    o_ref[0] = jnp.dot(cond, w_ref[0], preferred_element_type=F32,
                       precision=lax.Precision.HIGHEST) + b_ref[0]


def _modulation(c, w_ada, b_ada):
    B, D = c.shape
    n_out = w_ada.shape[-1]
    tn = 1536
    return pl.pallas_call(
        _mod_kernel,
        out_shape=jax.ShapeDtypeStruct((DEPTH, B, n_out), F32),
        grid=(DEPTH, n_out // tn),
        in_specs=[
            pl.BlockSpec((B, D), lambda l, j: (0, 0)),
            pl.BlockSpec((1, D, tn), lambda l, j: (l, 0, j)),
            pl.BlockSpec((1, 1, tn), lambda l, j: (l, 0, j)),
        ],
        out_specs=pl.BlockSpec((1, B, tn), lambda l, j: (l, 0, j)),
        compiler_params=pltpu.CompilerParams(
            dimension_semantics=("arbitrary", "arbitrary"), vmem_limit_bytes=VMEM_LIMIT),
        name="adaln_mod",
    )(c, w_ada, b_ada.reshape(DEPTH, 1, n_out))


def _inproj_kernel(x_ref, g_ref, sc_ref, sh_ref, w_ref, o_ref):
    x = x_ref[0]
    y = x * lax.rsqrt(jnp.mean(x * x, axis=-1, keepdims=True) + NORM_EPS) * g_ref[...]
    h = y * (1.0 + sc_ref[0]) + sh_ref[0]
    o_ref[0] = _dot(h.astype(BF16), w_ref[...])


def _in_projection(x, gain, scale, shift, w_bf16):
    B, S, D = x.shape
    N = w_bf16.shape[-1]
    tm = 256
    return pl.pallas_call(
        _inproj_kernel,
        out_shape=jax.ShapeDtypeStruct((B, S, N), F32),
        grid=(B, S // tm),
        in_specs=[
            pl.BlockSpec((1, tm, D), lambda b, i: (b, i, 0)),
            pl.BlockSpec((1, D), lambda b, i: (0, 0)),
            pl.BlockSpec((1, 1, D), lambda b, i: (b, 0, 0)),
            pl.BlockSpec((1, 1, D), lambda b, i: (b, 0, 0)),
            pl.BlockSpec((D, N), lambda b, i: (0, 0)),
        ],
        out_specs=pl.BlockSpec((1, tm, N), lambda b, i: (b, i, 0)),
        compiler_params=pltpu.CompilerParams(
            dimension_semantics=("arbitrary", "arbitrary"), vmem_limit_bytes=VMEM_LIMIT),
        name="in_proj",
    )(x, gain.reshape(1, D), scale, shift, w_bf16)


ATT_KEYS = 3 * N_SIDE


def _att_kernel(q_ref, k_ref, v_ref, o_ref, bias_ref, ob_ref, lse_ref):
    S = q_ref.shape[1]
    pair = pl.program_id(1)
    lane = lax.broadcasted_iota(jnp.int32, (1, LANES), 1)
    in_head = (lane < HEAD_DIM, lane >= HEAD_DIM)

    qi = lax.broadcasted_iota(jnp.int32, (N_SIDE, ATT_KEYS), 0)
    kj = lax.broadcasted_iota(jnp.int32, (N_SIDE, ATT_KEYS), 1)
    for br, (_, dil) in enumerate(DILATED_CONFIGS):
        for case in range(3):
            steps = jnp.abs(qi + N_SIDE * case - kj)
            dist = (dil * steps).astype(F32)
            for h in range(2):
                head = (2 * pair + h + 1).astype(F32) + jnp.zeros((1, ATT_KEYS), F32)
                slope = jnp.exp2(-8.0 * head / (ATT_WIDTH // HEAD_DIM))
                bias_ref[br, case, h] = jnp.where(steps <= N_SIDE, -slope * dist, NEG_INF)

    for br, (_, dil) in enumerate(DILATED_CONFIGS):
        sub_len = S // dil
        n_blocks = sub_len // N_SIDE

        def class_body(c, carry, br=br, dil=dil, sub_len=sub_len, n_blocks=n_blocks):
            def block_body(n, carry2):
                win = jnp.clip(N_SIDE * n - N_SIDE, 0, sub_len - ATT_KEYS)
                case = (N_SIDE * n - win) // N_SIDE
                if dil == 1:
                    q_rows = pl.ds(pl.multiple_of(N_SIDE * n, N_SIDE), N_SIDE)
                    k_rows = pl.ds(pl.multiple_of(win, N_SIDE), ATT_KEYS)
                else:
                    q_rows = pl.ds(c + dil * N_SIDE * n, N_SIDE, stride=dil)
                    k_rows = pl.ds(c + dil * win, ATT_KEYS, stride=dil)
                q = q_ref[0, q_rows, :] * (HEAD_DIM ** -0.5)
                kw = k_ref[0, k_rows, :]
                vw = v_ref[0, k_rows, :]
                outs, lses = [], []
                for h in range(2):
                    qh = jnp.where(in_head[h], q, 0.0)
                    s = _dot(qh, kw, NT_DIMS) + bias_ref[br, case, h]
                    m = jnp.max(s, axis=-1, keepdims=True)
                    p = jnp.exp(s - m)
                    l = jnp.sum(p, axis=-1, keepdims=True)
                    outs.append(_dot(p, vw) / l)
                    lses.append(m + jnp.log(l))
                ob_ref[br, q_rows, :] = jnp.where(in_head[0], outs[0], outs[1])
                lse_ref[br, q_rows, :] = jnp.where(in_head[0], lses[0], lses[1])
                return carry2
            return lax.fori_loop(0, n_blocks, block_body, carry)

        lax.fori_loop(0, dil, class_body, 0)

    tile = 512

    def merge_body(t, carry):
        rows = pl.ds(pl.multiple_of(t * tile, tile), tile)
        l0, l1, l2 = lse_ref[0, rows, :], lse_ref[1, rows, :], lse_ref[2, rows, :]
        m = jnp.maximum(jnp.maximum(l0, l1), l2)
        e0, e1, e2 = jnp.exp(l0 - m), jnp.exp(l1 - m), jnp.exp(l2 - m)
        num = e0 * ob_ref[0, rows, :] + e1 * ob_ref[1, rows, :] + e2 * ob_ref[2, rows, :]
        o_ref[0, rows, :] = num / (e0 + e1 + e2)
        return carry

    lax.fori_loop(0, S // tile, merge_body, 0)


def _attention(proj):
    B, S, _ = proj.shape
    n_pairs = ATT_WIDTH // LANES
    n_br = len(DILATED_CONFIGS)

    def col(blk):
        return pl.BlockSpec((1, S, LANES), lambda b, p: (b, 0, blk + p))

    return pl.pallas_call(
        _att_kernel,
        out_shape=jax.ShapeDtypeStruct((B, S, ATT_WIDTH), F32),
        grid=(B, n_pairs),
        in_specs=[col(ATT_Q_BLK), col(ATT_K_BLK), col(ATT_V_BLK)],
        out_specs=pl.BlockSpec((1, S, LANES), lambda b, p: (b, 0, p)),
        scratch_shapes=[
            pltpu.VMEM((n_br, 3, 2, N_SIDE, ATT_KEYS), F32),
            pltpu.VMEM((n_br, S, LANES), F32),
            pltpu.VMEM((n_br, S, LANES), F32),
        ],
        compiler_params=pltpu.CompilerParams(
            dimension_semantics=("arbitrary", "arbitrary"), vmem_limit_bytes=VMEM_LIMIT),
        name="dilated_attention",
    )(proj, proj, proj)


def _ret_kernel(q_ref, k_ref, v_ref, g_ref, gn_ref, o_ref, acc_ref):
    S = q_ref.shape[1]
    C = RET_CHUNK
    n_chunks = S // C
    pair = pl.program_id(1)
    lane = lax.broadcasted_iota(jnp.int32, (1, LANES), 1)
    in_head = (lane < HEAD_DIM, lane >= HEAD_DIM)
    block = _head_block_mask((LANES, LANES))
    hsum = _head_sum_matrix()

    def log_gamma(head_f32):
        return jnp.log1p(-jnp.exp2(-5.0 - head_f32))

    zeros_row = jnp.zeros((1, LANES), F32)
    lg_head = [log_gamma((2 * pair + h).astype(F32) + zeros_row) for h in range(2)]
    lg = jnp.where(in_head[0], lg_head[0], lg_head[1])
    pos = lax.broadcasted_iota(jnp.int32, (C, LANES), 0).astype(F32)
    k_fwd = jnp.exp(lg * (C - 1.0 - pos))
    k_bwd = jnp.exp(lg * pos)
    q_fwd = jnp.exp(lg * (pos + 1.0))
    q_bwd = jnp.exp(lg * (C - pos))
    state_decay = jnp.where(block, jnp.exp(lg * C), 0.0)
    ri = lax.broadcasted_iota(jnp.int32, (C, C), 0)
    ci = lax.broadcasted_iota(jnp.int32, (C, C), 1)
    dist = jnp.abs(ri - ci).astype(F32)
    intra = [jnp.exp(lg_head[h] * dist) for h in range(2)]

    def load(n):
        rows = pl.ds(pl.multiple_of(n * C, C), C)
        return rows, q_ref[0, rows, :], k_ref[0, rows, :] * (HEAD_DIM ** -0.5), v_ref[0, rows, :]

    def fwd_body(n, state):
        rows, q, k, v = load(n)
        out = _dot(q * q_fwd, state)
        for h in range(2):
            scores = _dot(jnp.where(in_head[h], q, 0.0), k, NT_DIMS) * intra[h]
            out = out + _dot(scores, jnp.where(in_head[h], v, 0.0))
        acc_ref[rows, :] = out
        upd = _dot(k * k_fwd, v, TN_DIMS)
        return state_decay * state + jnp.where(block, upd, 0.0)

    lax.fori_loop(0, n_chunks, fwd_body, jnp.zeros((LANES, LANES), F32))

    gn = gn_ref[...]

    def bwd_body(i, state):
        n = n_chunks - 1 - i
        rows, q, k, v = load(n)
        out = acc_ref[rows, :] + _dot(q * q_bwd, state)
        hi, lo = _split2(out)
        mu = (_dot(hi, hsum) + _dot(lo, hsum)) * (1.0 / HEAD_DIM)
        cen = out - mu
        hi, lo = _split2(cen * cen)
        var = (_dot(hi, hsum) + _dot(lo, hsum)) * (1.0 / HEAD_DIM)
        g = g_ref[0, rows, :]
        o_ref[0, rows, :] = cen * lax.rsqrt(var + NORM_EPS) * gn * (g * _sigmoid(g))
        upd = _dot(k * k_bwd, v, TN_DIMS)
        return state_decay * state + jnp.where(block, upd, 0.0)

    lax.fori_loop(0, n_chunks, bwd_body, jnp.zeros((LANES, LANES), F32))


def _retention(proj, ret_gn):
    B, S, _ = proj.shape
    n_pairs = RET_WIDTH // LANES

    def col(blk):
        return pl.BlockSpec((1, S, LANES), lambda b, p: (b, 0, blk + p))

    return pl.pallas_call(
        _ret_kernel,
        out_shape=jax.ShapeDtypeStruct((B, S, RET_WIDTH), F32),
        grid=(B, n_pairs),
        in_specs=[col(RET_Q_BLK), col(RET_K_BLK), col(RET_V_BLK), col(RET_G_BLK),
                  pl.BlockSpec((1, LANES), lambda b, p: (0, p))],
        out_specs=pl.BlockSpec((1, S, LANES), lambda b, p: (b, 0, p)),
        scratch_shapes=[pltpu.VMEM((S, LANES), F32)],
        compiler_params=pltpu.CompilerParams(
            dimension_semantics=("arbitrary", "arbitrary"), vmem_limit_bytes=VMEM_LIMIT),
        name="retention",
    )(proj, proj, proj, proj, ret_gn.reshape(1, RET_WIDTH))


HGRN_TILE = 128


def _hgrn_kernel(layer, q_ref, zf_ref, zb_ref, i_ref, lbl_ref, o_ref,
                 b_s, kk_s, v_s, w_s):
    S = q_ref.shape[1]
    T, C = HGRN_TILE, HGRN_CHUNK
    n_tiles = S // T
    chunks = T // C
    block = _head_block_mask((LANES, LANES))
    hsum = _head_sum_matrix()

    logits = [lbl_ref[l:l + 1, :] for l in range(DEPTH)]
    mx = functools.reduce(jnp.maximum, logits)
    ex = [jnp.exp(t - mx) for t in logits]
    den = functools.reduce(lambda a, b: a + b, ex)
    probs = [e / den for e in ex]
    cum = functools.reduce(lambda a, b: a + b, probs[:layer + 1])
    lb = jnp.clip(cum - probs[0], 0.0, 1.0 - 1e-6)
    log_lb = jnp.log(jnp.maximum(lb, LB_FLOOR))
    log_one_minus_lb = jnp.log1p(-lb)

    ri = lax.broadcasted_iota(jnp.int32, (T, T), 0)
    ci = lax.broadcasted_iota(jnp.int32, (T, T), 1)
    chunk_bits = C.bit_length() - 1
    same_chunk = jnp.right_shift(ri, chunk_bits) == jnp.right_shift(ci, chunk_bits)
    tri = (jnp.where(same_chunk & (ci <= ri), 1.0, 0.0).astype(BF16),
           jnp.where(same_chunk & (ci >= ri), 1.0, 0.0).astype(BF16))
    ones_chunk = jnp.where(same_chunk, 1.0, 0.0).astype(BF16)
    t_in_chunk = lax.broadcasted_iota(jnp.int32, (C, LANES), 0)

    def exact_sum(mat, parts):
        return _dot(mat, parts[0]) + _dot(mat, parts[1]) + _dot(mat, parts[2])

    def tile_step(rows, z_ref, backward, state):
        z = z_ref[0, rows, :]
        hq = q_ref[0, rows, :]
        q = hq * _sigmoid(hq)
        v = i_ref[0, rows, :]
        log_sig = jnp.minimum(z, 0.0) - jnp.log1p(jnp.exp(-jnp.abs(z)))
        a1 = log_one_minus_lb + log_sig
        log_f = jnp.maximum(log_lb, a1) + jnp.log1p(jnp.exp(-jnp.abs(log_lb - a1)))
        kk = (1.0 - lb) * (1.0 / (1.0 + jnp.exp(z)))
        parts = _split3(log_f)
        b = exact_sum(tri[1] if backward else tri[0], parts)
        b_tot = exact_sum(ones_chunk, parts)
        qe = q * jnp.exp(b)
        ke = kk * jnp.exp(b_tot - b)
        decay = jnp.exp(b_tot)
        b_s[...] = b
        kk_s[...] = kk
        v_s[...] = v

        order = range(chunks - 1, -1, -1) if backward else range(chunks)
        outs = [None] * chunks
        for c in order:
            lo = c * C
            qc, bc = q[lo:lo + C], b[lo:lo + C]
            for s in range(C):
                b_src = jnp.broadcast_to(b_s[lo + s:lo + s + 1, :], (C, LANES))
                k_src = jnp.broadcast_to(kk_s[lo + s:lo + s + 1, :], (C, LANES))
                w_s[s * C:(s + 1) * C, :] = (qc * jnp.exp(jnp.minimum(bc - b_src, 0.0)) * k_src).astype(BF16)
            attn = _dot(w_s[...], hsum)
            out = _dot(qe[lo:lo + C], state, NT_DIMS)
            for s in range(C):
                keep = (t_in_chunk <= s) if backward else (t_in_chunk >= s)
                v_src = jnp.broadcast_to(v_s[lo + s:lo + s + 1, :], (C, LANES))
                out = out + jnp.where(keep, attn[s * C:(s + 1) * C], 0.0) * v_src
            outs[c] = out
            upd = _dot(v[lo:lo + C], ke[lo:lo + C], TN_DIMS)
            state = state * decay[lo:lo + 1, :] + jnp.where(block, upd, 0.0)
        return jnp.concatenate(outs, axis=0), state

    def fwd_body(t, state):
        rows = pl.ds(pl.multiple_of(t * T, T), T)
        out, state = tile_step(rows, zf_ref, False, state)
        o_ref[0, rows, :] = out
        return state

    lax.fori_loop(0, n_tiles, fwd_body, jnp.zeros((LANES, LANES), F32))

    def bwd_body(i, state):
        rows = pl.ds(pl.multiple_of((n_tiles - 1 - i) * T, T), T)
        out, state = tile_step(rows, zb_ref, True, state)
        o_ref[0, rows, :] = o_ref[0, rows, :] + out
        return state

    lax.fori_loop(0, n_tiles, bwd_body, jnp.zeros((LANES, LANES), F32))


def _hgrn(proj, lb_logits, layer):
    B, S, _ = proj.shape
    n_pairs = HGRN_WIDTH // LANES

    def col(blk):
        return pl.BlockSpec((1, S, LANES), lambda b, p: (b, 0, blk + p))

    return pl.pallas_call(
        functools.partial(_hgrn_kernel, layer),
        out_shape=jax.ShapeDtypeStruct((B, S, HGRN_WIDTH), F32),
        grid=(B, n_pairs),
        in_specs=[col(HG_Q_BLK), col(HG_ZF_BLK), col(HG_ZB_BLK), col(HG_I_BLK),
                  pl.BlockSpec((DEPTH, LANES), lambda b, p: (0, p))],
        out_specs=pl.BlockSpec((1, S, LANES), lambda b, p: (b, 0, p)),
        scratch_shapes=[
            pltpu.VMEM((HGRN_TILE, LANES), F32),
            pltpu.VMEM((HGRN_TILE, LANES), F32),
            pltpu.VMEM((HGRN_TILE, LANES), F32),
            pltpu.VMEM((HGRN_CHUNK * HGRN_CHUNK, LANES), BF16),
        ],
        compiler_params=pltpu.CompilerParams(
            dimension_semantics=("arbitrary", "arbitrary"), vmem_limit_bytes=VMEM_LIMIT),
        name="hgrn2",
    )(proj, proj, proj, proj, lb_logits)


def _ffn_kernel(final_norm, att_ref, ret_ref, hg_ref, hga_ref, hgb_ref, x_ref, wo_ref, hgn_ref,
                gate1_ref, gffn_ref, sc2_ref, sh2_ref, gate2_ref, wg_ref, wu_ref, wd_ref, gfin_ref,
                o_ref, x2_s, h_s, acc_s):
    j = pl.program_id(2)

    @pl.when(j == 0)
    def _():
        hraw = hg_ref[0]
        hgate = jnp.concatenate([hga_ref[0], hgb_ref[0]], axis=-1)
        hgo = (hraw * lax.rsqrt(jnp.mean(hraw * hraw, axis=-1, keepdims=True) + NORM_EPS)
               * hgn_ref[...] * (hgate * _sigmoid(hgate)))
        y = (_dot(att_ref[0].astype(BF16), wo_ref[0:ATT_WIDTH, :])
             + _dot(ret_ref[0].astype(BF16), wo_ref[ATT_WIDTH:ATT_WIDTH + RET_WIDTH, :])
             + _dot(hgo.astype(BF16), wo_ref[ATT_WIDTH + RET_WIDTH:, :]))
        x2 = x_ref[0] + gate1_ref[0] * y
        x2_s[...] = x2
        n = x2 * lax.rsqrt(jnp.mean(x2 * x2, axis=-1, keepdims=True) + NORM_EPS) * gffn_ref[...]
        h_s[...] = (n * (1.0 + sc2_ref[0]) + sh2_ref[0]).astype(BF16)
        acc_s[...] = jnp.zeros_like(acc_s)

    h = h_s[...]
    g = _dot(h, wg_ref[...])
    u = _dot(h, wu_ref[...])
    a = (g * _sigmoid(g) * u).astype(BF16)
    acc_s[...] += _dot(a, wd_ref[...])

    @pl.when(j == pl.num_programs(2) - 1)
    def _():
        out = x2_s[...] + gate2_ref[0] * acc_s[...]
        if final_norm:
            out = out * lax.rsqrt(jnp.mean(out * out, axis=-1, keepdims=True) + NORM_EPS) * gfin_ref[...]
        o_ref[0] = out


def _out_proj_ffn(att, ret, hg_raw, proj, x, wo, hgrn_gn, gate1, g_ffn, scale2, shift2, gate2,
                  w_gate_up, w_down, g_final, final_norm):
    B, S, D = x.shape
    tm = 512
    n_h = 2
    th = FFN_HIDDEN // n_h

    def rows(width):
        return pl.BlockSpec((1, tm, width), lambda b, i, j: (b, i, 0))

    def per_batch():
        return pl.BlockSpec((1, 1, D), lambda b, i, j: (b, 0, 0))

    def vec(width):
        return pl.BlockSpec((1, width), lambda b, i, j: (0, 0))

    return pl.pallas_call(
        functools.partial(_ffn_kernel, final_norm),
        out_shape=jax.ShapeDtypeStruct((B, S, D), F32),
        grid=(B, S // tm, n_h),
        in_specs=[
            rows(ATT_WIDTH), rows(RET_WIDTH), rows(HGRN_WIDTH),
            pl.BlockSpec((1, tm, LANES), lambda b, i, j: (b, i, HG_G_BLK)),
            pl.BlockSpec((1, tm, LANES), lambda b, i, j: (b, i, HG_G_BLK + 1)),
            rows(D),
            pl.BlockSpec((D, D), lambda b, i, j: (0, 0)),
            vec(HGRN_WIDTH),
            per_batch(), vec(D), per_batch(), per_batch(), per_batch(),
            pl.BlockSpec((D, th), lambda b, i, j: (0, j)),
            pl.BlockSpec((D, th), lambda b, i, j: (0, j + n_h)),
            pl.BlockSpec((th, D), lambda b, i, j: (j, 0)),
            vec(D),
        ],
        out_specs=pl.BlockSpec((1, tm, D), lambda b, i, j: (b, i, 0)),
        scratch_shapes=[
            pltpu.VMEM((tm, D), F32),
            pltpu.VMEM((tm, D), BF16),
            pltpu.VMEM((tm, D), F32),
        ],
        compiler_params=pltpu.CompilerParams(
            dimension_semantics=("arbitrary", "arbitrary", "arbitrary"), vmem_limit_bytes=VMEM_LIMIT),
        name="out_proj_ffn",
    )(att, ret, hg_raw, proj, proj, x, wo, hgrn_gn.reshape(1, HGRN_WIDTH), gate1,
      g_ffn.reshape(1, D), scale2, shift2, gate2, w_gate_up, w_gate_up, w_down, g_final.reshape(1, D))


def kernel(x, c, w_ada, b_ada, g_mix, w_in, ret_gn, hgrn_gn, hgrn_lb_logits, w_out,
           g_ffn, w_gate_up, w_down, g_final):
    B, S, D = x.shape
    mod = _modulation(c, w_ada, b_ada).reshape(DEPTH, B, N_MOD, 1, D)
    for layer in range(DEPTH):
        shift1, scale1, gate1, shift2, scale2, gate2 = (mod[layer, :, i] for i in range(N_MOD))
        proj = _in_projection(x, g_mix[layer], scale1, shift1, w_in[layer].astype(BF16))
        att = _attention(proj)
        ret = _retention(proj, ret_gn[layer])
        hg_raw = _hgrn(proj, hgrn_lb_logits, layer)
        x = _out_proj_ffn(att, ret, hg_raw, proj, x, w_out[layer].astype(BF16), hgrn_gn[layer],
                          gate1, g_ffn[layer], scale2, shift2, gate2,
                          w_gate_up[layer].astype(BF16), w_down[layer].astype(BF16), g_final,
                          final_norm=(layer == DEPTH - 1))
    return x
```

```python
import functools

import jax
import jax.numpy as jnp
from jax import lax
from jax.experimental import pallas as pl
from jax.experimental.pallas import tpu as pltpu

D_MODEL = 1024
DEPTH = 2
HEAD_DIM = 64
ATT_WIDTH = 384
RET_WIDTH = 384
HGRN_WIDTH = 256
IN_WIDTH = 3 * ATT_WIDTH + 4 * RET_WIDTH + 5 * HGRN_WIDTH
DILATED_CONFIGS = ((128, 1), (512, 4), (2048, 16))
N_SIDE = 64
RET_CHUNK = 128
HGRN_CHUNK = 16
FFN_HIDDEN = 2816
N_MOD = 6
NORM_EPS = 1e-6
NEG_INF = -1e30
LB_FLOOR = 1e-30

LANES = 128
ATT_Q_BLK, ATT_K_BLK, ATT_V_BLK = 0, 3, 6
RET_Q_BLK, RET_K_BLK, RET_V_BLK, RET_G_BLK = 9, 12, 15, 18
HG_Q_BLK, HG_ZF_BLK, HG_ZB_BLK, HG_I_BLK, HG_G_BLK = 21, 23, 25, 27, 29

VMEM_LIMIT = 56 * 1024 * 1024

F32 = jnp.float32
BF16 = jnp.bfloat16

NT_DIMS = (((1,), (1,)), ((), ()))
TN_DIMS = (((0,), (0,)), ((), ()))


def _dot(a, b, dims=None):
    if dims is None:
        return jnp.dot(a, b, preferred_element_type=F32)
    return lax.dot_general(a, b, dims, preferred_element_type=F32)


def _split2(x):
    hi = x.astype(BF16)
    lo = (x - hi.astype(F32)).astype(BF16)
    return hi, lo


def _split3(x):
    hi = x.astype(BF16)
    r1 = x - hi.astype(F32)
    mid = r1.astype(BF16)
    lo = (r1 - mid.astype(F32)).astype(BF16)
    return hi, mid, lo


def _head_block_mask(shape):
    r = lax.broadcasted_iota(jnp.int32, shape, 0)
    c = lax.broadcasted_iota(jnp.int32, shape, 1)
    return (r >= HEAD_DIM) == (c >= HEAD_DIM)


def _head_sum_matrix():
    return jnp.where(_head_block_mask((LANES, LANES)), 1.0, 0.0).astype(BF16)


def _sigmoid(x):
    return 1.0 / (1.0 + jnp.exp(-x))


def _mod_kernel(c_ref, w_ref, b_ref, o_ref):
    c = c_ref[...]
    cond = c * _sigmoid(c)
    o_ref[0] = jnp.dot(cond, w_ref[0], preferred_element_type=F32,
                       precision=lax.Precision.HIGHEST) + b_ref[0]


def _modulation(c, w_ada, b_ada):
    B, D = c.shape
    n_out = w_ada.shape[-1]
    tn = 1536
    return pl.pallas_call(
        _mod_kernel,
        out_shape=jax.ShapeDtypeStruct((DEPTH, B, n_out), F32),
        grid=(DEPTH, n_out // tn),
        in_specs=[
            pl.BlockSpec((B, D), lambda l, j: (0, 0)),
            pl.BlockSpec((1, D, tn), lambda l, j: (l, 0, j)),
            pl.BlockSpec((1, 1, tn), lambda l, j: (l, 0, j)),
        ],
        out_specs=pl.BlockSpec((1, B, tn), lambda l, j: (l, 0, j)),
        compiler_params=pltpu.CompilerParams(
            dimension_semantics=("arbitrary", "arbitrary"), vmem_limit_bytes=VMEM_LIMIT),
        name="adaln_mod",
    )(c, w_ada, b_ada.reshape(DEPTH, 1, n_out))


def _inproj_kernel(x_ref, g_ref, sc_ref, sh_ref, w_ref, o_ref):
    x = x_ref[0]
    y = x * lax.rsqrt(jnp.mean(x * x, axis=-1, keepdims=True) + NORM_EPS) * g_ref[...]
    h = y * (1.0 + sc_ref[0]) + sh_ref[0]
    o_ref[0] = _dot(h.astype(BF16), w_ref[...])


def _in_projection(x, gain, scale, shift, w_bf16):
    B, S, D = x.shape
    N = w_bf16.shape[-1]
    tm = 256
    return pl.pallas_call(
        _inproj_kernel,
        out_shape=jax.ShapeDtypeStruct((B, S, N), F32),
        grid=(B, S // tm),
        in_specs=[
            pl.BlockSpec((1, tm, D), lambda b, i: (b, i, 0)),
            pl.BlockSpec((1, D), lambda b, i: (0, 0)),
            pl.BlockSpec((1, 1, D), lambda b, i: (b, 0, 0)),
            pl.BlockSpec((1, 1, D), lambda b, i: (b, 0, 0)),
            pl.BlockSpec((D, N), lambda b, i: (0, 0)),
        ],
        out_specs=pl.BlockSpec((1, tm, N), lambda b, i: (b, i, 0)),
        compiler_params=pltpu.CompilerParams(
            dimension_semantics=("arbitrary", "arbitrary"), vmem_limit_bytes=VMEM_LIMIT),
        name="in_proj",
    )(x, gain.reshape(1, D), scale, shift, w_bf16)


ATT_KEYS = 3 * N_SIDE
ATT_UNROLL = 8
RET_UNROLL = 4


def _att_kernel(q_ref, k_ref, v_ref, o_ref, bias_ref, ob_ref, lse_ref):
    S = q_ref.shape[1]
    pair = pl.program_id(1)
    lane = lax.broadcasted_iota(jnp.int32, (1, LANES), 1)
    in_head = (lane < HEAD_DIM, lane >= HEAD_DIM)

    qi = lax.broadcasted_iota(jnp.int32, (N_SIDE, ATT_KEYS), 0)
    kj = lax.broadcasted_iota(jnp.int32, (N_SIDE, ATT_KEYS), 1)
    for br, (_, dil) in enumerate(DILATED_CONFIGS):
        for case in range(3):
            steps = jnp.abs(qi + N_SIDE * case - kj)
            dist = (dil * steps).astype(F32)
            for h in range(2):
                head = (2 * pair + h + 1).astype(F32) + jnp.zeros((1, ATT_KEYS), F32)
                slope = jnp.exp2(-8.0 * head / (ATT_WIDTH // HEAD_DIM))
                bias_ref[br, case, h] = jnp.where(steps <= N_SIDE, -slope * dist, NEG_INF)

    for br, (_, dil) in enumerate(DILATED_CONFIGS):
        sub_len = S // dil
        n_blocks = sub_len // N_SIDE

        blk_bits = n_blocks.bit_length() - 1

        def group_body(it, carry, br=br, dil=dil, sub_len=sub_len, n_blocks=n_blocks, blk_bits=blk_bits):
            rows, scores, vws = [], [], []
            for u in range(ATT_UNROLL):
                idx = it * ATT_UNROLL + u
                c = jnp.right_shift(idx, blk_bits)
                n = jnp.bitwise_and(idx, n_blocks - 1)
                win = jnp.clip(N_SIDE * n - N_SIDE, 0, sub_len - ATT_KEYS)
                case = jnp.right_shift(N_SIDE * n - win, 6)
                if dil == 1:
                    q_rows = pl.ds(pl.multiple_of(N_SIDE * n, N_SIDE), N_SIDE)
                    k_rows = pl.ds(pl.multiple_of(win, N_SIDE), ATT_KEYS)
                else:
                    q_rows = pl.ds(c + dil * N_SIDE * n, N_SIDE, stride=dil)
                    k_rows = pl.ds(c + dil * win, ATT_KEYS, stride=dil)
                q = q_ref[0, q_rows, :] * (HEAD_DIM ** -0.5)
                kw = k_ref[0, k_rows, :]
                rows.append(q_rows)
                vws.append(v_ref[0, k_rows, :])
                for h in range(2):
                    qh = jnp.where(in_head[h], q, 0.0)
                    scores.append(_dot(qh, kw, NT_DIMS) + bias_ref[br, case, h])
            probs, lsums, lses = [], [], []
            for s in scores:
                m = jnp.max(s, axis=-1, keepdims=True)
                p = jnp.exp(s - m)
                l = jnp.sum(p, axis=-1, keepdims=True)
                probs.append(p)
                lsums.append(l)
                lses.append(m + jnp.log(l))
            outs = [_dot(p, vws[i // 2]) for i, p in enumerate(probs)]
            for u in range(ATT_UNROLL):
                o0, o1 = outs[2 * u] / lsums[2 * u], outs[2 * u + 1] / lsums[2 * u + 1]
                ob_ref[br, rows[u], :] = jnp.where(in_head[0], o0, o1)
                lse_ref[br, rows[u], :] = jnp.where(in_head[0], lses[2 * u], lses[2 * u + 1])
            return carry

        lax.fori_loop(0, dil * n_blocks // ATT_UNROLL, group_body, 0)

    tile = 512

    def merge_body(t, carry):
        rows = pl.ds(pl.multiple_of(t * tile, tile), tile)
        l0, l1, l2 = lse_ref[0, rows, :], lse_ref[1, rows, :], lse_ref[2, rows, :]
        m = jnp.maximum(jnp.maximum(l0, l1), l2)
        e0, e1, e2 = jnp.exp(l0 - m), jnp.exp(l1 - m), jnp.exp(l2 - m)
        num = e0 * ob_ref[0, rows, :] + e1 * ob_ref[1, rows, :] + e2 * ob_ref[2, rows, :]
        o_ref[0, rows, :] = num / (e0 + e1 + e2)
        return carry

    lax.fori_loop(0, S // tile, merge_body, 0)


def _attention(proj):
    B, S, _ = proj.shape
    n_pairs = ATT_WIDTH // LANES
    n_br = len(DILATED_CONFIGS)

    def col(blk):
        return pl.BlockSpec((1, S, LANES), lambda b, p: (b, 0, blk + p))

    return pl.pallas_call(
        _att_kernel,
        out_shape=jax.ShapeDtypeStruct((B, S, ATT_WIDTH), F32),
        grid=(B, n_pairs),
        in_specs=[col(ATT_Q_BLK), col(ATT_K_BLK), col(ATT_V_BLK)],
        out_specs=pl.BlockSpec((1, S, LANES), lambda b, p: (b, 0, p)),
        scratch_shapes=[
            pltpu.VMEM((n_br, 3, 2, N_SIDE, ATT_KEYS), F32),
            pltpu.VMEM((n_br, S, LANES), F32),
            pltpu.VMEM((n_br, S, LANES), F32),
        ],
        compiler_params=pltpu.CompilerParams(
            dimension_semantics=("arbitrary", "arbitrary"), vmem_limit_bytes=VMEM_LIMIT),
        name="dilated_attention",
    )(proj, proj, proj)


def _ret_kernel(q_ref, k_ref, v_ref, g_ref, gn_ref, o_ref, sf_ref, sb_ref):
    S = q_ref.shape[1]
    C = RET_CHUNK
    n_chunks = S // C
    pair = pl.program_id(1)
    lane = lax.broadcasted_iota(jnp.int32, (1, LANES), 1)
    in_head = (lane < HEAD_DIM, lane >= HEAD_DIM)
    block = _head_block_mask((LANES, LANES))
    hsum = _head_sum_matrix()

    def log_gamma(head_f32):
        return jnp.log1p(-jnp.exp2(-5.0 - head_f32))

    zeros_row = jnp.zeros((1, LANES), F32)
    lg_head = [log_gamma((2 * pair + h).astype(F32) + zeros_row) for h in range(2)]
    lg = jnp.where(in_head[0], lg_head[0], lg_head[1])
    pos = lax.broadcasted_iota(jnp.int32, (C, LANES), 0).astype(F32)
    k_fwd = jnp.exp(lg * (C - 1.0 - pos))
    k_bwd = jnp.exp(lg * pos)
    q_fwd = jnp.exp(lg * (pos + 1.0))
    q_bwd = jnp.exp(lg * (C - pos))
    state_decay = jnp.where(block, jnp.exp(lg * C), 0.0)
    ri = lax.broadcasted_iota(jnp.int32, (C, C), 0)
    ci = lax.broadcasted_iota(jnp.int32, (C, C), 1)
    dist = jnp.abs(ri - ci).astype(F32)
    intra = [jnp.exp(lg_head[h] * dist) for h in range(2)]

    def chunk_rows(n):
        return pl.ds(pl.multiple_of(n * C, C), C)

    def incr_body(it, carry):
        for u in range(RET_UNROLL):
            n = it * RET_UNROLL + u
            rows = chunk_rows(n)
            k = k_ref[0, rows, :] * (HEAD_DIM ** -0.5)
            v = v_ref[0, rows, :]
            sf_ref[n] = jnp.where(block, _dot(k * k_fwd, v, TN_DIMS), 0.0)
            sb_ref[n] = jnp.where(block, _dot(k * k_bwd, v, TN_DIMS), 0.0)
        return carry

    lax.fori_loop(0, n_chunks // RET_UNROLL, incr_body, 0)

    def scan_fwd(n, state):
        incr = sf_ref[n]
        sf_ref[n] = state
        return state_decay * state + incr

    def scan_bwd(i, state):
        n = n_chunks - 1 - i
        incr = sb_ref[n]
        sb_ref[n] = state
        return state_decay * state + incr

    lax.fori_loop(0, n_chunks, scan_fwd, jnp.zeros((LANES, LANES), F32), unroll=4)
    lax.fori_loop(0, n_chunks, scan_bwd, jnp.zeros((LANES, LANES), F32), unroll=4)

    gn = gn_ref[...]

    def out_body(it, carry):
        rows = [chunk_rows(it * RET_UNROLL + u) for u in range(RET_UNROLL)]
        qs = [q_ref[0, r, :] for r in rows]
        vs = [v_ref[0, r, :] for r in rows]
        scores = []
        for u in range(RET_UNROLL):
            k = k_ref[0, rows[u], :] * (HEAD_DIM ** -0.5)
            scores.append([_dot(jnp.where(in_head[h], qs[u], 0.0), k, NT_DIMS) for h in range(2)])
        outs = []
        for u in range(RET_UNROLL):
            n = it * RET_UNROLL + u
            outs.append(_dot(qs[u] * q_fwd, sf_ref[n]) + _dot(qs[u] * q_bwd, sb_ref[n]))
        for u in range(RET_UNROLL):
            for h in range(2):
                outs[u] = outs[u] + _dot(scores[u][h] * intra[h], jnp.where(in_head[h], vs[u], 0.0))
        mus = []
        for u in range(RET_UNROLL):
            hi, lo = _split2(outs[u])
            mus.append((_dot(hi, hsum) + _dot(lo, hsum)) * (1.0 / HEAD_DIM))
        cens = [outs[u] - mus[u] for u in range(RET_UNROLL)]
        for u in range(RET_UNROLL):
            hi, lo = _split2(cens[u] * cens[u])
            var = (_dot(hi, hsum) + _dot(lo, hsum)) * (1.0 / HEAD_DIM)
            g = g_ref[0, rows[u], :]
            o_ref[0, rows[u], :] = cens[u] * lax.rsqrt(var + NORM_EPS) * gn * (g * _sigmoid(g))
        return carry

    lax.fori_loop(0, n_chunks // RET_UNROLL, out_body, 0)


def _retention(proj, ret_gn):
    B, S, _ = proj.shape
    n_pairs = RET_WIDTH // LANES

    def col(blk):
        return pl.BlockSpec((1, S, LANES), lambda b, p: (b, 0, blk + p))

    return pl.pallas_call(
        _ret_kernel,
        out_shape=jax.ShapeDtypeStruct((B, S, RET_WIDTH), F32),
        grid=(B, n_pairs),
        in_specs=[col(RET_Q_BLK), col(RET_K_BLK), col(RET_V_BLK), col(RET_G_BLK),
                  pl.BlockSpec((1, LANES), lambda b, p: (0, p))],
        out_specs=pl.BlockSpec((1, S, LANES), lambda b, p: (b, 0, p)),
        scratch_shapes=[pltpu.VMEM((S // RET_CHUNK, LANES, LANES), F32),
                        pltpu.VMEM((S // RET_CHUNK, LANES, LANES), F32)],
        compiler_params=pltpu.CompilerParams(
            dimension_semantics=("arbitrary", "arbitrary"), vmem_limit_bytes=VMEM_LIMIT),
        name="retention",
    )(proj, proj, proj, proj, ret_gn.reshape(1, RET_WIDTH))


HGRN_TILE = 128


def _hgrn_kernel(layer, q_ref, zf_ref, zb_ref, i_ref, lbl_ref, o_ref,
                 b_s, kk_s, v_s, w_s):
    S = q_ref.shape[1]
    T, C = HGRN_TILE, HGRN_CHUNK
    n_tiles = S // T
    chunks = T // C
    block = _head_block_mask((LANES, LANES))
    hsum = _head_sum_matrix()

    logits = [lbl_ref[l:l + 1, :] for l in range(DEPTH)]
    mx = functools.reduce(jnp.maximum, logits)
    ex = [jnp.exp(t - mx) for t in logits]
    den = functools.reduce(lambda a, b: a + b, ex)
    probs = [e / den for e in ex]
    cum = functools.reduce(lambda a, b: a + b, probs[:layer + 1])
    lb = jnp.clip(cum - probs[0], 0.0, 1.0 - 1e-6)
    log_lb = jnp.log(jnp.maximum(lb, LB_FLOOR))
    log_one_minus_lb = jnp.log1p(-lb)

    ri = lax.broadcasted_iota(jnp.int32, (T, T), 0)
    ci = lax.broadcasted_iota(jnp.int32, (T, T), 1)
    chunk_bits = C.bit_length() - 1
    same_chunk = jnp.right_shift(ri, chunk_bits) == jnp.right_shift(ci, chunk_bits)
    tri = (jnp.where(same_chunk & (ci <= ri), 1.0, 0.0).astype(BF16),
           jnp.where(same_chunk & (ci >= ri), 1.0, 0.0).astype(BF16))
    ones_chunk = jnp.where(same_chunk, 1.0, 0.0).astype(BF16)
    t_in_chunk = lax.broadcasted_iota(jnp.int32, (C, LANES), 0)

    def exact_sum(mat, parts):
        return _dot(mat, parts[0]) + _dot(mat, parts[1]) + _dot(mat, parts[2])

    def tile_step(rows, z_ref, backward, state):
        z = z_ref[0, rows, :]
        hq = q_ref[0, rows, :]
        q = hq * _sigmoid(hq)
        v = i_ref[0, rows, :]
        log_sig = jnp.minimum(z, 0.0) - jnp.log1p(jnp.exp(-jnp.abs(z)))
        a1 = log_one_minus_lb + log_sig
        log_f = jnp.maximum(log_lb, a1) + jnp.log1p(jnp.exp(-jnp.abs(log_lb - a1)))
        kk = (1.0 - lb) * (1.0 / (1.0 + jnp.exp(z)))
        parts = _split3(log_f)
        b = exact_sum(tri[1] if backward else tri[0], parts)
        b_tot = exact_sum(ones_chunk, parts)
        qe = q * jnp.exp(b)
        ke = kk * jnp.exp(b_tot - b)
        decay = jnp.exp(b_tot)
        b_s[...] = b
        kk_s[...] = kk
        v_s[...] = v

        order = range(chunks - 1, -1, -1) if backward else range(chunks)
        upds = [_dot(v[c * C:(c + 1) * C], ke[c * C:(c + 1) * C], TN_DIMS) for c in range(chunks)]
        states = [None] * chunks
        for c in order:
            states[c] = state
            state = state * decay[c * C:c * C + 1, :] + jnp.where(block, upds[c], 0.0)
        outs = [_dot(qe[c * C:(c + 1) * C], states[c], NT_DIMS) for c in range(chunks)]

        for src in range(T):
            lo = src // C * C
            b_src = jnp.broadcast_to(b_s[src:src + 1, :], (C, LANES))
            k_src = jnp.broadcast_to(kk_s[src:src + 1, :], (C, LANES))
            w = q[lo:lo + C] * jnp.exp(jnp.minimum(b[lo:lo + C] - b_src, 0.0)) * k_src
            w_s[src * C:(src + 1) * C, :] = w.astype(BF16)
        attn = _dot(w_s[...], hsum)

        for c in range(chunks):
            lo = c * C
            for s in range(C):
                keep = (t_in_chunk <= s) if backward else (t_in_chunk >= s)
                v_src = jnp.broadcast_to(v_s[lo + s:lo + s + 1, :], (C, LANES))
                outs[c] = outs[c] + jnp.where(keep, attn[(lo + s) * C:(lo + s + 1) * C], 0.0) * v_src
        return jnp.concatenate(outs, axis=0), state

    def fwd_body(t, state):
        rows = pl.ds(pl.multiple_of(t * T, T), T)
        out, state = tile_step(rows, zf_ref, False, state)
        o_ref[0, rows, :] = out
        return state

    lax.fori_loop(0, n_tiles, fwd_body, jnp.zeros((LANES, LANES), F32))

    def bwd_body(i, state):
        rows = pl.ds(pl.multiple_of((n_tiles - 1 - i) * T, T), T)
        out, state = tile_step(rows, zb_ref, True, state)
        o_ref[0, rows, :] = o_ref[0, rows, :] + out
        return state

    lax.fori_loop(0, n_tiles, bwd_body, jnp.zeros((LANES, LANES), F32))


def _hgrn(proj, lb_logits, layer):
    B, S, _ = proj.shape
    n_pairs = HGRN_WIDTH // LANES

    def col(blk):
        return pl.BlockSpec((1, S, LANES), lambda b, p: (b, 0, blk + p))

    return pl.pallas_call(
        functools.partial(_hgrn_kernel, layer),
        out_shape=jax.ShapeDtypeStruct((B, S, HGRN_WIDTH), F32),
        grid=(B, n_pairs),
        in_specs=[col(HG_Q_BLK), col(HG_ZF_BLK), col(HG_ZB_BLK), col(HG_I_BLK),
                  pl.BlockSpec((DEPTH, LANES), lambda b, p: (0, p))],
        out_specs=pl.BlockSpec((1, S, LANES), lambda b, p: (b, 0, p)),
        scratch_shapes=[
            pltpu.VMEM((HGRN_TILE, LANES), F32),
            pltpu.VMEM((HGRN_TILE, LANES), F32),
            pltpu.VMEM((HGRN_TILE, LANES), F32),
            pltpu.VMEM((HGRN_TILE * HGRN_CHUNK, LANES), BF16),
        ],
        compiler_params=pltpu.CompilerParams(
            dimension_semantics=("arbitrary", "arbitrary"), vmem_limit_bytes=VMEM_LIMIT),
        name="hgrn2",
    )(proj, proj, proj, proj, lb_logits)


def _ffn_kernel(final_norm, att_ref, ret_ref, hg_ref, hga_ref, hgb_ref, x_ref, wo_ref, hgn_ref,
                gate1_ref, gffn_ref, sc2_ref, sh2_ref, gate2_ref, wg_ref, wu_ref, wd_ref, gfin_ref,
                o_ref, x2_s, h_s, acc_s):
    j = pl.program_id(2)

    @pl.when(j == 0)
    def _():
        hraw = hg_ref[0]
        hgate = jnp.concatenate([hga_ref[0], hgb_ref[0]], axis=-1)
        hgo = (hraw * lax.rsqrt(jnp.mean(hraw * hraw, axis=-1, keepdims=True) + NORM_EPS)
               * hgn_ref[...] * (hgate * _sigmoid(hgate)))
        y = (_dot(att_ref[0].astype(BF16), wo_ref[0:ATT_WIDTH, :])
             + _dot(ret_ref[0].astype(BF16), wo_ref[ATT_WIDTH:ATT_WIDTH + RET_WIDTH, :])
             + _dot(hgo.astype(BF16), wo_ref[ATT_WIDTH + RET_WIDTH:, :]))
        x2 = x_ref[0] + gate1_ref[0] * y
        x2_s[...] = x2
        n = x2 * lax.rsqrt(jnp.mean(x2 * x2, axis=-1, keepdims=True) + NORM_EPS) * gffn_ref[...]
        h_s[...] = (n * (1.0 + sc2_ref[0]) + sh2_ref[0]).astype(BF16)
        acc_s[...] = jnp.zeros_like(acc_s)

    h = h_s[...]
    g = _dot(h, wg_ref[...])
    u = _dot(h, wu_ref[...])
    a = (g * _sigmoid(g) * u).astype(BF16)
    acc_s[...] += _dot(a, wd_ref[...])

    @pl.when(j == pl.num_programs(2) - 1)
    def _():
        out = x2_s[...] + gate2_ref[0] * acc_s[...]
        if final_norm:
            out = out * lax.rsqrt(jnp.mean(out * out, axis=-1, keepdims=True) + NORM_EPS) * gfin_ref[...]
        o_ref[0] = out


def _out_proj_ffn(att, ret, hg_raw, proj, x, wo, hgrn_gn, gate1, g_ffn, scale2, shift2, gate2,
                  w_gate_up, w_down, g_final, final_norm):
    B, S, D = x.shape
    tm = 512
    n_h = 2
    th = FFN_HIDDEN // n_h

    def rows(width):
        return pl.BlockSpec((1, tm, width), lambda b, i, j: (b, i, 0))

    def per_batch():
        return pl.BlockSpec((1, 1, D), lambda b, i, j: (b, 0, 0))

    def vec(width):
        return pl.BlockSpec((1, width), lambda b, i, j: (0, 0))

    return pl.pallas_call(
        functools.partial(_ffn_kernel, final_norm),
        out_shape=jax.ShapeDtypeStruct((B, S, D), F32),
        grid=(B, S // tm, n_h),
        in_specs=[
            rows(ATT_WIDTH), rows(RET_WIDTH), rows(HGRN_WIDTH),
            pl.BlockSpec((1, tm, LANES), lambda b, i, j: (b, i, HG_G_BLK)),
            pl.BlockSpec((1, tm, LANES), lambda b, i, j: (b, i, HG_G_BLK + 1)),
            rows(D),
            pl.BlockSpec((D, D), lambda b, i, j: (0, 0)),
            vec(HGRN_WIDTH),
            per_batch(), vec(D), per_batch(), per_batch(), per_batch(),
            pl.BlockSpec((D, th), lambda b, i, j: (0, j)),
            pl.BlockSpec((D, th), lambda b, i, j: (0, j + n_h)),
            pl.BlockSpec((th, D), lambda b, i, j: (j, 0)),
            vec(D),
        ],
        out_specs=pl.BlockSpec((1, tm, D), lambda b, i, j: (b, i, 0)),
        scratch_shapes=[
            pltpu.VMEM((tm, D), F32),
            pltpu.VMEM((tm, D), BF16),
            pltpu.VMEM((tm, D), F32),
        ],
        compiler_params=pltpu.CompilerParams(
            dimension_semantics=("arbitrary", "arbitrary", "arbitrary"), vmem_limit_bytes=VMEM_LIMIT),
        name="out_proj_ffn",
    )(att, ret, hg_raw, proj, proj, x, wo, hgrn_gn.reshape(1, HGRN_WIDTH), gate1,
      g_ffn.reshape(1, D), scale2, shift2, gate2, w_gate_up, w_gate_up, w_down, g_final.reshape(1, D))


def kernel(x, c, w_ada, b_ada, g_mix, w_in, ret_gn, hgrn_gn, hgrn_lb_logits, w_out,
           g_ffn, w_gate_up, w_down, g_final):
    B, S, D = x.shape
    mod = _modulation(c, w_ada, b_ada).reshape(DEPTH, B, N_MOD, 1, D)
    for layer in range(DEPTH):
        shift1, scale1, gate1, shift2, scale2, gate2 = (mod[layer, :, i] for i in range(N_MOD))
        proj = _in_projection(x, g_mix[layer], scale1, shift1, w_in[layer].astype(BF16))
        att = _attention(proj)
        ret = _retention(proj, ret_gn[layer])
        hg_raw = _hgrn(proj, hgrn_lb_logits, layer)
        x = _out_proj_ffn(att, ret, hg_raw, proj, x, w_out[layer].astype(BF16), hgrn_gn[layer],
                          gate1, g_ffn[layer], scale2, shift2, gate2,
                          w_gate_up[layer].astype(BF16), w_down[layer].astype(BF16), g_final,
                          final_norm=(layer == DEPTH - 1))
    return x
```

```python
import functools

import jax
import jax.numpy as jnp
from jax import lax
from jax.experimental import pallas as pl
from jax.experimental.pallas import tpu as pltpu

D_MODEL = 1024
DEPTH = 2
HEAD_DIM = 64
ATT_WIDTH = 384
RET_WIDTH = 384
HGRN_WIDTH = 256
IN_WIDTH = 3 * ATT_WIDTH + 4 * RET_WIDTH + 5 * HGRN_WIDTH
DILATED_CONFIGS = ((128, 1), (512, 4), (2048, 16))
N_SIDE = 64
RET_CHUNK = 128
HGRN_CHUNK = 16
FFN_HIDDEN = 2816
N_MOD = 6
NORM_EPS = 1e-6
NEG_INF = -1e30
LB_FLOOR = 1e-30

LANES = 128
SUBLANES = 8
LOG2E = 1.4426950408889634
ATT_Q_BLK, ATT_K_BLK, ATT_V_BLK = 0, 3, 6
RET_Q_BLK, RET_K_BLK, RET_V_BLK, RET_G_BLK = 9, 12, 15, 18
HG_Q_BLK, HG_ZF_BLK, HG_ZB_BLK, HG_I_BLK, HG_G_BLK = 21, 23, 25, 27, 29

VMEM_LIMIT = 56 * 1024 * 1024

F32 = jnp.float32
BF16 = jnp.bfloat16

NT_DIMS = (((1,), (1,)), ((), ()))
TN_DIMS = (((0,), (0,)), ((), ()))


def _dot(a, b, dims=None):
    if dims is None:
        return jnp.dot(a, b, preferred_element_type=F32)
    return lax.dot_general(a, b, dims, preferred_element_type=F32)


def _split2(x):
    hi = x.astype(BF16)
    lo = (x - hi.astype(F32)).astype(BF16)
    return hi, lo


def _split3(x):
    hi = x.astype(BF16)
    r1 = x - hi.astype(F32)
    mid = r1.astype(BF16)
    lo = (r1 - mid.astype(F32)).astype(BF16)
    return hi, mid, lo


def _head_block_mask(shape):
    r = lax.broadcasted_iota(jnp.int32, shape, 0)
    c = lax.broadcasted_iota(jnp.int32, shape, 1)
    return (r >= HEAD_DIM) == (c >= HEAD_DIM)


def _head_sum_matrix():
    return jnp.where(_head_block_mask((LANES, LANES)), 1.0, 0.0).astype(BF16)


def _sigmoid(x):
    return 1.0 / (1.0 + jnp.exp(-x))


def _mod_kernel(c_ref, w_ref, b_ref, o_ref):
    c = c_ref[...]
    cond = c * _sigmoid(c)
    o_ref[0] = jnp.dot(cond, w_ref[0], preferred_element_type=F32,
                       precision=lax.Precision.HIGHEST) + b_ref[0]


def _modulation(c, w_ada, b_ada):
    B, D = c.shape
    n_out = w_ada.shape[-1]
    tn = 1536
    return pl.pallas_call(
        _mod_kernel,
        out_shape=jax.ShapeDtypeStruct((DEPTH, B, n_out), F32),
        grid=(DEPTH, n_out // tn),
        in_specs=[
            pl.BlockSpec((B, D), lambda l, j: (0, 0)),
            pl.BlockSpec((1, D, tn), lambda l, j: (l, 0, j)),
            pl.BlockSpec((1, 1, tn), lambda l, j: (l, 0, j)),
        ],
        out_specs=pl.BlockSpec((1, B, tn), lambda l, j: (l, 0, j)),
        compiler_params=pltpu.CompilerParams(
            dimension_semantics=("arbitrary", "arbitrary"), vmem_limit_bytes=VMEM_LIMIT),
        name="adaln_mod",
    )(c, w_ada, b_ada.reshape(DEPTH, 1, n_out))


def _inproj_kernel(x_ref, g_ref, sc_ref, sh_ref, w_ref, o_ref):
    x = x_ref[0]
    y = x * lax.rsqrt(jnp.mean(x * x, axis=-1, keepdims=True) + NORM_EPS) * g_ref[...]
    h = y * (1.0 + sc_ref[0]) + sh_ref[0]
    o_ref[0] = _dot(h.astype(BF16), w_ref[...])


def _in_projection(x, gain, scale, shift, w_bf16):
    B, S, D = x.shape
    N = w_bf16.shape[-1]
    tm = 256
    return pl.pallas_call(
        _inproj_kernel,
        out_shape=jax.ShapeDtypeStruct((B, S, N), F32),
        grid=(B, S // tm),
        in_specs=[
            pl.BlockSpec((1, tm, D), lambda b, i: (b, i, 0)),
            pl.BlockSpec((1, D), lambda b, i: (0, 0)),
            pl.BlockSpec((1, 1, D), lambda b, i: (b, 0, 0)),
            pl.BlockSpec((1, 1, D), lambda b, i: (b, 0, 0)),
            pl.BlockSpec((D, N), lambda b, i: (0, 0)),
        ],
        out_specs=pl.BlockSpec((1, tm, N), lambda b, i: (b, i, 0)),
        compiler_params=pltpu.CompilerParams(
            dimension_semantics=("arbitrary", "arbitrary"), vmem_limit_bytes=VMEM_LIMIT),
        name="in_proj",
    )(x, gain.reshape(1, D), scale, shift, w_bf16)


ATT_KEYS = 3 * N_SIDE
ATT_UNROLL = 4
RET_UNROLL = 4


def _att_kernel(q_ref, k_ref, v_ref, o_ref, bias_ref, qd_ref, kd_ref, vd_ref, t4_ref, ob_ref, lse_ref,
                s_ref, p_ref, rl_ref, ml_ref):
    S = q_ref.shape[1]
    pair = pl.program_id(1)
    lane = lax.broadcasted_iota(jnp.int32, (1, LANES), 1)
    in_head = (lane < HEAD_DIM, lane >= HEAD_DIM)
    lane_q = lax.broadcasted_iota(jnp.int32, (N_SIDE, LANES), 1)
    q_in_head = (lane_q < HEAD_DIM, lane_q >= HEAD_DIM)

    qi = lax.broadcasted_iota(jnp.int32, (N_SIDE, ATT_KEYS), 0)
    kj = lax.broadcasted_iota(jnp.int32, (N_SIDE, ATT_KEYS), 1)
    for br, (_, dil) in enumerate(DILATED_CONFIGS):
        for case in range(3):
            steps = jnp.abs(qi + N_SIDE * case - kj)
            dist = (dil * steps).astype(F32)
            for h in range(2):
                head = (2 * pair + h + 1).astype(F32) + jnp.zeros((1, ATT_KEYS), F32)
                slope = jnp.exp2(-8.0 * head / (ATT_WIDTH // HEAD_DIM)) * LOG2E
                bias_ref[br, case, h * N_SIDE:(h + 1) * N_SIDE, :] = jnp.where(
                    steps <= N_SIDE, -slope * dist, NEG_INF)

    assert [d for _, d in DILATED_CONFIGS] == [1, 4, 16]
    piece = 256
    quarter = S // 4
    for src_ref, dst_ref, scale in ((q_ref, qd_ref, HEAD_DIM ** -0.5 * LOG2E), (k_ref, kd_ref, 1.0),
                                    (v_ref, vd_ref, 1.0)):
        def pass1(j, carry, src_ref=src_ref, dst_ref=dst_ref, scale=scale):
            dst = pl.ds(pl.multiple_of(j * piece, piece), piece)
            dst_ref[0, dst, :] = (src_ref[0, dst, :] * scale).astype(BF16)
            cls = j // (quarter // piece)
            first = (j % (quarter // piece)) * piece
            x = src_ref[0, pl.ds(cls + 4 * first, piece, stride=4), :] * scale
            t4_ref[dst, :] = x
            dst_ref[1, dst, :] = x.astype(BF16)
            return carry

        lax.fori_loop(0, S // piece, pass1, 0)

        def pass2(j, carry, dst_ref=dst_ref):
            dst = pl.ds(pl.multiple_of(j * piece, piece), piece)
            x = t4_ref[pl.ds((j % 4) * quarter + j // 4, piece, stride=4), :]
            dst_ref[2, dst, :] = x.astype(BF16)
            return carry

        lax.fori_loop(0, S // piece, pass2, 0)

    zero_q = jnp.zeros((N_SIDE, LANES), BF16)
    for br, (_, dil) in enumerate(DILATED_CONFIGS):
        sub_len = S // dil
        n_blocks = sub_len // N_SIDE
        blk_bits = n_blocks.bit_length() - 1

        n_groups = dil * n_blocks // ATT_UNROLL

        def block_index(g, u, n_blocks=n_blocks, blk_bits=blk_bits, sub_len=sub_len):
            idx = g * ATT_UNROLL + u
            c = jnp.right_shift(idx, blk_bits)
            n = jnp.bitwise_and(idx, n_blocks - 1)
            win = jnp.clip(N_SIDE * n - N_SIDE, 0, sub_len - ATT_KEYS)
            case = jnp.right_shift(N_SIDE * n - win, 6)
            q_src = pl.ds(pl.multiple_of(N_SIDE * idx, N_SIDE), N_SIDE)
            k_src = pl.ds(pl.multiple_of(N_SIDE * (idx - n) + win, N_SIDE), ATT_KEYS)
            return c, n, case, q_src, k_src

        def scores_stage(g, slot, br=br):
            for u in range(ATT_UNROLL):
                _, _, case, q_src, k_src = block_index(g, u)
                q = qd_ref[br, q_src, :]
                q2 = jnp.concatenate([jnp.where(q_in_head[h], q, zero_q) for h in range(2)], axis=0)
                s_ref[slot, u] = _dot(q2, kd_ref[br, k_src, :], NT_DIMS) + bias_ref[br, case]

        def softmax_stage(slot):
            for u in range(ATT_UNROLL):
                s = s_ref[slot, u]
                m = jnp.max(s, axis=-1, keepdims=True)
                p = jnp.exp2(s - m)
                l = jnp.sum(p, axis=-1, keepdims=True)
                p_ref[slot, u] = p.astype(BF16)
                rl_ref[slot, u] = jnp.broadcast_to(1.0 / l, (2 * N_SIDE, LANES))
                ml_ref[slot, u] = jnp.broadcast_to(m + jnp.log2(l), (2 * N_SIDE, LANES))

        def values_stage(g, slot, br=br, dil=dil):
            for u in range(ATT_UNROLL):
                c, n, _, q_src, k_src = block_index(g, u)
                rows = q_src if dil == 1 else pl.ds(c + dil * N_SIDE * n, N_SIDE, stride=dil)
                o = _dot(p_ref[slot, u], vd_ref[br, k_src, :]) * rl_ref[slot, u]
                ml = ml_ref[slot, u]
                ob_ref[br, rows, :] = jnp.where(in_head[0], o[:N_SIDE], o[N_SIDE:])
                lse_ref[br, rows, :] = jnp.where(in_head[0], ml[:N_SIDE], ml[N_SIDE:])

        def trip(i, carry):
            for par in range(2):
                g = 2 * i + par
                values_stage(g - 2, par)
                softmax_stage(1 - par)
                scores_stage(g, par)
            return carry

        scores_stage(0, 0)
        softmax_stage(0)
        scores_stage(1, 1)
        lax.fori_loop(1, n_groups // 2, trip, 0)
        values_stage(n_groups - 2, 0)
        softmax_stage(1)
        values_stage(n_groups - 1, 1)

    tile = 512

    def merge_body(t, carry):
        rows = pl.ds(pl.multiple_of(t * tile, tile), tile)
        l0, l1, l2 = lse_ref[0, rows, :], lse_ref[1, rows, :], lse_ref[2, rows, :]
        m = jnp.maximum(jnp.maximum(l0, l1), l2)
        e0, e1, e2 = jnp.exp2(l0 - m), jnp.exp2(l1 - m), jnp.exp2(l2 - m)
        num = e0 * ob_ref[0, rows, :] + e1 * ob_ref[1, rows, :] + e2 * ob_ref[2, rows, :]
        o_ref[0, rows, :] = num / (e0 + e1 + e2)
        return carry

    lax.fori_loop(0, S // tile, merge_body, 0)


def _attention(proj):
    B, S, _ = proj.shape
    n_pairs = ATT_WIDTH // LANES
    n_br = len(DILATED_CONFIGS)

    def col(blk):
        return pl.BlockSpec((1, S, LANES), lambda b, p: (b, 0, blk + p))

    return pl.pallas_call(
        _att_kernel,
        out_shape=jax.ShapeDtypeStruct((B, S, ATT_WIDTH), F32),
        grid=(B, n_pairs),
        in_specs=[col(ATT_Q_BLK), col(ATT_K_BLK), col(ATT_V_BLK)],
        out_specs=pl.BlockSpec((1, S, LANES), lambda b, p: (b, 0, p)),
        scratch_shapes=[
            pltpu.VMEM((n_br, 3, 2 * N_SIDE, ATT_KEYS), F32),
            pltpu.VMEM((n_br, S, LANES), BF16),
            pltpu.VMEM((n_br, S, LANES), BF16),
            pltpu.VMEM((n_br, S, LANES), BF16),
            pltpu.VMEM((S, LANES), F32),
            pltpu.VMEM((n_br, S, LANES), F32),
            pltpu.VMEM((n_br, S, LANES), F32),
            pltpu.VMEM((2, ATT_UNROLL, 2 * N_SIDE, ATT_KEYS), F32),
            pltpu.VMEM((2, ATT_UNROLL, 2 * N_SIDE, ATT_KEYS), BF16),
            pltpu.VMEM((2, ATT_UNROLL, 2 * N_SIDE, LANES), F32),
            pltpu.VMEM((2, ATT_UNROLL, 2 * N_SIDE, LANES), F32),
        ],
        compiler_params=pltpu.CompilerParams(
            dimension_semantics=("arbitrary", "arbitrary"), vmem_limit_bytes=VMEM_LIMIT),
        name="dilated_attention",
    )(proj, proj, proj)


def _ret_kernel(q_ref, k_ref, v_ref, g_ref, gn_ref, o_ref, sf_ref, sb_ref):
    S = q_ref.shape[1]
    C = RET_CHUNK
    n_chunks = S // C
    pair = pl.program_id(1)
    lane = lax.broadcasted_iota(jnp.int32, (1, LANES), 1)
    in_head = (lane < HEAD_DIM, lane >= HEAD_DIM)
    block = _head_block_mask((LANES, LANES))
    hsum = _head_sum_matrix()

    def log_gamma(head_f32):
        return jnp.log1p(-jnp.exp2(-5.0 - head_f32))

    zeros_row = jnp.zeros((1, LANES), F32)
    lg_head = [log_gamma((2 * pair + h).astype(F32) + zeros_row) for h in range(2)]
    lg = jnp.where(in_head[0], lg_head[0], lg_head[1])
    pos = lax.broadcasted_iota(jnp.int32, (C, LANES), 0).astype(F32)
    k_fwd = jnp.exp(lg * (C - 1.0 - pos))
    k_bwd = jnp.exp(lg * pos)
    q_fwd = jnp.exp(lg * (pos + 1.0))
    q_bwd = jnp.exp(lg * (C - pos))
    state_decay = jnp.where(block, jnp.exp(lg * C), 0.0)
    ri = lax.broadcasted_iota(jnp.int32, (C, C), 0)
    ci = lax.broadcasted_iota(jnp.int32, (C, C), 1)
    dist = jnp.abs(ri - ci).astype(F32)
    intra = [jnp.exp(lg_head[h] * dist) for h in range(2)]

    def chunk_rows(n):
        return pl.ds(pl.multiple_of(n * C, C), C)

    def incr_body(it, carry):
        for u in range(RET_UNROLL):
            n = it * RET_UNROLL + u
            rows = chunk_rows(n)
            k = k_ref[0, rows, :] * (HEAD_DIM ** -0.5)
            v = v_ref[0, rows, :]
            sf_ref[n] = jnp.where(block, _dot(k * k_fwd, v, TN_DIMS), 0.0)
            sb_ref[n] = jnp.where(block, _dot(k * k_bwd, v, TN_DIMS), 0.0)
        return carry

    lax.fori_loop(0, n_chunks // RET_UNROLL, incr_body, 0)

    def scan_fwd(n, state):
        incr = sf_ref[n]
        sf_ref[n] = state
        return state_decay * state + incr

    def scan_bwd(i, state):
        n = n_chunks - 1 - i
        incr = sb_ref[n]
        sb_ref[n] = state
        return state_decay * state + incr

    lax.fori_loop(0, n_chunks, scan_fwd, jnp.zeros((LANES, LANES), F32), unroll=4)
    lax.fori_loop(0, n_chunks, scan_bwd, jnp.zeros((LANES, LANES), F32), unroll=4)

    gn = gn_ref[...]

    def out_body(it, carry):
        rows = [chunk_rows(it * RET_UNROLL + u) for u in range(RET_UNROLL)]
        qs = [q_ref[0, r, :] for r in rows]
        vs = [v_ref[0, r, :] for r in rows]
        scores = []
        for u in range(RET_UNROLL):
            k = k_ref[0, rows[u], :] * (HEAD_DIM ** -0.5)
            scores.append([_dot(jnp.where(in_head[h], qs[u], 0.0), k, NT_DIMS) for h in range(2)])
        outs = []
        for u in range(RET_UNROLL):
            n = it * RET_UNROLL + u
            outs.append(_dot(qs[u] * q_fwd, sf_ref[n]) + _dot(qs[u] * q_bwd, sb_ref[n]))
        for u in range(RET_UNROLL):
            for h in range(2):
                outs[u] = outs[u] + _dot(scores[u][h] * intra[h], jnp.where(in_head[h], vs[u], 0.0))
        mus = []
        for u in range(RET_UNROLL):
            hi, lo = _split2(outs[u])
            mus.append((_dot(hi, hsum) + _dot(lo, hsum)) * (1.0 / HEAD_DIM))
        cens = [outs[u] - mus[u] for u in range(RET_UNROLL)]
        for u in range(RET_UNROLL):
            hi, lo = _split2(cens[u] * cens[u])
            var = (_dot(hi, hsum) + _dot(lo, hsum)) * (1.0 / HEAD_DIM)
            g = g_ref[0, rows[u], :]
            o_ref[0, rows[u], :] = cens[u] * lax.rsqrt(var + NORM_EPS) * gn * (g * _sigmoid(g))
        return carry

    lax.fori_loop(0, n_chunks // RET_UNROLL, out_body, 0)


def _retention(proj, ret_gn):
    B, S, _ = proj.shape
    n_pairs = RET_WIDTH // LANES

    def col(blk):
        return pl.BlockSpec((1, S, LANES), lambda b, p: (b, 0, blk + p))

    return pl.pallas_call(
        _ret_kernel,
        out_shape=jax.ShapeDtypeStruct((B, S, RET_WIDTH), F32),
        grid=(B, n_pairs),
        in_specs=[col(RET_Q_BLK), col(RET_K_BLK), col(RET_V_BLK), col(RET_G_BLK),
                  pl.BlockSpec((1, LANES), lambda b, p: (0, p))],
        out_specs=pl.BlockSpec((1, S, LANES), lambda b, p: (b, 0, p)),
        scratch_shapes=[pltpu.VMEM((S // RET_CHUNK, LANES, LANES), F32),
                        pltpu.VMEM((S // RET_CHUNK, LANES, LANES), F32)],
        compiler_params=pltpu.CompilerParams(
            dimension_semantics=("arbitrary", "arbitrary"), vmem_limit_bytes=VMEM_LIMIT),
        name="retention",
    )(proj, proj, proj, proj, ret_gn.reshape(1, RET_WIDTH))


HGRN_TILE = 128
HGRN_SLABS = HGRN_TILE * (HGRN_CHUNK // SUBLANES) * 3 // 4


def _hgrn_kernel(layer, q_ref, zf_ref, zb_ref, i_ref, lbl_ref, o_ref,
                 qt_s, b_s, c_s, v_s, dec_s, qe_s, ke_s, w_s):
    S = q_ref.shape[1]
    T, C = HGRN_TILE, HGRN_CHUNK
    n_tiles = S // T
    chunks = T // C
    block = _head_block_mask((LANES, LANES))
    hsum = _head_sum_matrix()

    logits = [lbl_ref[l:l + 1, :] for l in range(DEPTH)]
    mx = functools.reduce(jnp.maximum, logits)
    ex = [jnp.exp(t - mx) for t in logits]
    den = functools.reduce(lambda a, b: a + b, ex)
    probs = [e / den for e in ex]
    cum = functools.reduce(lambda a, b: a + b, probs[:layer + 1])
    lb = jnp.clip(cum - probs[0], 0.0, 1.0 - 1e-6)
    lb_floor = jnp.maximum(lb, LB_FLOOR)
    one_minus_lb = 1.0 - lb

    ri = lax.broadcasted_iota(jnp.int32, (T, T), 0)
    ci = lax.broadcasted_iota(jnp.int32, (T, T), 1)
    chunk_bits = C.bit_length() - 1
    same_chunk = jnp.right_shift(ri, chunk_bits) == jnp.right_shift(ci, chunk_bits)
    tri = (jnp.where(same_chunk & (ci <= ri), 1.0, 0.0).astype(BF16),
           jnp.where(same_chunk & (ci >= ri), 1.0, 0.0).astype(BF16))
    ones_chunk = jnp.where(same_chunk, 1.0, 0.0).astype(BF16)

    def exact_sum(mat, parts):
        return _dot(mat, parts[0]) + _dot(mat, parts[1]) + _dot(mat, parts[2])

    t_loc = lax.broadcasted_iota(jnp.int32, (SUBLANES, LANES), 0)
    half_masks = {False: [jnp.where(t_loc >= s, 0.0, NEG_INF) for s in range(SUBLANES)],
                  True: [jnp.where(t_loc <= s, 0.0, NEG_INF) for s in range(SUBLANES)]}

    def slab_list(backward):
        slabs = []
        for c in range(chunks):
            for s in range(C):
                for half in range(C // SUBLANES):
                    if (half <= s // SUBLANES) if backward else (half >= s // SUBLANES):
                        slabs.append((c, s, half))
        return slabs

    slabs_of = {False: slab_list(False), True: slab_list(True)}

    def tile_rows(step, backward):
        tile = (n_tiles - 1 - step) if backward else step
        if isinstance(tile, int):
            return pl.ds(tile * T, T)
        return pl.ds(pl.multiple_of(tile * T, T), T)

    def prepare(step, backward, slot):
        rows = tile_rows(step, backward)
        z = (zb_ref if backward else zf_ref)[0, rows, :]
        hq = q_ref[0, rows, :]
        q = hq * _sigmoid(hq)
        e = jnp.exp(-jnp.abs(z))
        r = 1.0 / (1.0 + e)
        er = e * r
        nonneg = z >= 0.0
        f = lb_floor + one_minus_lb * jnp.where(nonneg, r, er)
        kk = one_minus_lb * jnp.where(nonneg, er, r)
        parts = _split3(jnp.log(f))
        b2 = exact_sum(tri[1] if backward else tri[0], parts) * LOG2E
        b2_tot = exact_sum(ones_chunk, parts) * LOG2E
        qt_s[slot] = q
        b_s[slot] = b2
        c_s[slot] = b2 - jnp.log(kk) * LOG2E
        v_s[slot] = i_ref[0, rows, :]
        qe_s[slot] = (q * jnp.exp2(b2)).astype(BF16)
        ke_s[slot] = (kk * jnp.exp2(b2_tot - b2)).astype(BF16)
        dec_s[slot] = jnp.exp2(b2_tot)

    def advance(backward, slot, state):
        slabs = slabs_of[backward]
        per_chunk = len(slabs) // chunks
        attns = []
        for c in range(chunks):
            base = c * per_chunk
            for i in range(base, base + per_chunk, 2):
                ws = []
                for cc, s, half in slabs[i:i + 2]:
                    lo = cc * C + half * SUBLANES
                    c_src = jnp.broadcast_to(c_s[slot, cc * C + s:cc * C + s + 1, :], (SUBLANES, LANES))
                    expo = b_s[slot, lo:lo + SUBLANES, :] - c_src
                    if half == s // SUBLANES:
                        expo = expo + half_masks[backward][s % SUBLANES]
                    ws.append(qt_s[slot, lo:lo + SUBLANES, :] * jnp.exp2(expo))
                w_s[slot, i * SUBLANES:(i + 2) * SUBLANES, :] = jnp.concatenate(ws, axis=0).astype(BF16)
            attns.append(_dot(w_s[slot, base * SUBLANES:(base + per_chunk) * SUBLANES, :], hsum))

        order = range(chunks - 1, -1, -1) if backward else range(chunks)
        upds = [_dot(v_s[slot, c * C:(c + 1) * C, :].astype(BF16), ke_s[slot, c * C:(c + 1) * C, :], TN_DIMS)
                for c in range(chunks)]
        states = [None] * chunks
        for c in order:
            states[c] = state
            state = state * dec_s[slot, c * C:c * C + 1, :] + jnp.where(block, upds[c], 0.0)
        outs = [_dot(qe_s[slot, c * C:(c + 1) * C, :], states[c].astype(BF16), NT_DIMS) for c in range(chunks)]

        pieces = []
        for c in range(chunks):
            acc = [outs[c][h * SUBLANES:(h + 1) * SUBLANES] for h in range(C // SUBLANES)]
            for i in range(per_chunk):
                cc, s, half = slabs[c * per_chunk + i]
                v_src = jnp.broadcast_to(v_s[slot, cc * C + s:cc * C + s + 1, :], (SUBLANES, LANES))
                acc[half] = acc[half] + attns[c][i * SUBLANES:(i + 1) * SUBLANES] * v_src
            pieces.extend(acc)
        return jnp.concatenate(pieces, axis=0), state

    for backward in (False, True):
        def trip(i, state, backward=backward):
            out0, state = advance(backward, 0, state)
            prepare(2 * i + 1, backward, 1)
            out1, state = advance(backward, 1, state)
            prepare(jnp.minimum(2 * i + 2, n_tiles - 1), backward, 0)
            for step, out in ((2 * i, out0), (2 * i + 1, out1)):
                rows = tile_rows(step, backward)
                o_ref[0, rows, :] = (o_ref[0, rows, :] + out) if backward else out
            return state

        prepare(0, backward, 0)
        lax.fori_loop(0, n_tiles // 2, trip, jnp.zeros((LANES, LANES), F32))


def _hgrn(proj, lb_logits, layer):
    B, S, _ = proj.shape
    n_pairs = HGRN_WIDTH // LANES

    def col(blk):
        return pl.BlockSpec((1, S, LANES), lambda b, p: (b, 0, blk + p))

    return pl.pallas_call(
        functools.partial(_hgrn_kernel, layer),
        out_shape=jax.ShapeDtypeStruct((B, S, HGRN_WIDTH), F32),
        grid=(B, n_pairs),
        in_specs=[col(HG_Q_BLK), col(HG_ZF_BLK), col(HG_ZB_BLK), col(HG_I_BLK),
                  pl.BlockSpec((DEPTH, LANES), lambda b, p: (0, p))],
        out_specs=pl.BlockSpec((1, S, LANES), lambda b, p: (b, 0, p)),
        scratch_shapes=(
            [pltpu.VMEM((2, HGRN_TILE, LANES), F32)] * 5
            + [pltpu.VMEM((2, HGRN_TILE, LANES), BF16)] * 2
            + [pltpu.VMEM((2, HGRN_SLABS * SUBLANES, LANES), BF16)]),
        compiler_params=pltpu.CompilerParams(
            dimension_semantics=("arbitrary", "arbitrary"), vmem_limit_bytes=VMEM_LIMIT),
        name="hgrn2",
    )(proj, proj, proj, proj, lb_logits)


def _ffn_kernel(final_norm, att_ref, ret_ref, hg_ref, hga_ref, hgb_ref, x_ref, wo_ref, hgn_ref,
                gate1_ref, gffn_ref, sc2_ref, sh2_ref, gate2_ref, wg_ref, wu_ref, wd_ref, gfin_ref,
                o_ref, x2_s, h_s, acc_s):
    j = pl.program_id(2)

    @pl.when(j == 0)
    def _():
        hraw = hg_ref[0]
        hgate = jnp.concatenate([hga_ref[0], hgb_ref[0]], axis=-1)
        hgo = (hraw * lax.rsqrt(jnp.mean(hraw * hraw, axis=-1, keepdims=True) + NORM_EPS)
               * hgn_ref[...] * (hgate * _sigmoid(hgate)))
        y = (_dot(att_ref[0].astype(BF16), wo_ref[0:ATT_WIDTH, :])
             + _dot(ret_ref[0].astype(BF16), wo_ref[ATT_WIDTH:ATT_WIDTH + RET_WIDTH, :])
             + _dot(hgo.astype(BF16), wo_ref[ATT_WIDTH + RET_WIDTH:, :]))
        x2 = x_ref[0] + gate1_ref[0] * y
        x2_s[...] = x2
        n = x2 * lax.rsqrt(jnp.mean(x2 * x2, axis=-1, keepdims=True) + NORM_EPS) * gffn_ref[...]
        h_s[...] = (n * (1.0 + sc2_ref[0]) + sh2_ref[0]).astype(BF16)
        acc_s[...] = jnp.zeros_like(acc_s)

    h = h_s[...]
    g = _dot(h, wg_ref[...])
    u = _dot(h, wu_ref[...])
    a = (g * _sigmoid(g) * u).astype(BF16)
    acc_s[...] += _dot(a, wd_ref[...])

    @pl.when(j == pl.num_programs(2) - 1)
    def _():
        out = x2_s[...] + gate2_ref[0] * acc_s[...]
        if final_norm:
            out = out * lax.rsqrt(jnp.mean(out * out, axis=-1, keepdims=True) + NORM_EPS) * gfin_ref[...]
        o_ref[0] = out


def _out_proj_ffn(att, ret, hg_raw, proj, x, wo, hgrn_gn, gate1, g_ffn, scale2, shift2, gate2,
                  w_gate_up, w_down, g_final, final_norm):
    B, S, D = x.shape
    tm = 512
    n_h = 2
    th = FFN_HIDDEN // n_h

    def rows(width):
        return pl.BlockSpec((1, tm, width), lambda b, i, j: (b, i, 0))

    def per_batch():
        return pl.BlockSpec((1, 1, D), lambda b, i, j: (b, 0, 0))

    def vec(width):
        return pl.BlockSpec((1, width), lambda b, i, j: (0, 0))

    return pl.pallas_call(
        functools.partial(_ffn_kernel, final_norm),
        out_shape=jax.ShapeDtypeStruct((B, S, D), F32),
        grid=(B, S // tm, n_h),
        in_specs=[
            rows(ATT_WIDTH), rows(RET_WIDTH), rows(HGRN_WIDTH),
            pl.BlockSpec((1, tm, LANES), lambda b, i, j: (b, i, HG_G_BLK)),
            pl.BlockSpec((1, tm, LANES), lambda b, i, j: (b, i, HG_G_BLK + 1)),
            rows(D),
            pl.BlockSpec((D, D), lambda b, i, j: (0, 0)),
            vec(HGRN_WIDTH),
            per_batch(), vec(D), per_batch(), per_batch(), per_batch(),
            pl.BlockSpec((D, th), lambda b, i, j: (0, j)),
            pl.BlockSpec((D, th), lambda b, i, j: (0, j + n_h)),
            pl.BlockSpec((th, D), lambda b, i, j: (j, 0)),
            vec(D),
        ],
        out_specs=pl.BlockSpec((1, tm, D), lambda b, i, j: (b, i, 0)),
        scratch_shapes=[
            pltpu.VMEM((tm, D), F32),
            pltpu.VMEM((tm, D), BF16),
            pltpu.VMEM((tm, D), F32),
        ],
        compiler_params=pltpu.CompilerParams(
            dimension_semantics=("arbitrary", "arbitrary", "arbitrary"), vmem_limit_bytes=VMEM_LIMIT),
        name="out_proj_ffn",
    )(att, ret, hg_raw, proj, proj, x, wo, hgrn_gn.reshape(1, HGRN_WIDTH), gate1,
      g_ffn.reshape(1, D), scale2, shift2, gate2, w_gate_up, w_gate_up, w_down, g_final.reshape(1, D))


def kernel(x, c, w_ada, b_ada, g_mix, w_in, ret_gn, hgrn_gn, hgrn_lb_logits, w_out,
           g_ffn, w_gate_up, w_down, g_final):
    B, S, D = x.shape
    mod = _modulation(c, w_ada, b_ada).reshape(DEPTH, B, N_MOD, 1, D)
    for layer in range(DEPTH):
        shift1, scale1, gate1, shift2, scale2, gate2 = (mod[layer, :, i] for i in range(N_MOD))
        proj = _in_projection(x, g_mix[layer], scale1, shift1, w_in[layer].astype(BF16))
        att = _attention(proj)
        ret = _retention(proj, ret_gn[layer])
        hg_raw = _hgrn(proj, hgrn_lb_logits, layer)
        x = _out_proj_ffn(att, ret, hg_raw, proj, x, w_out[layer].astype(BF16), hgrn_gn[layer],
                          gate1, g_ffn[layer], scale2, shift2, gate2,
                          w_gate_up[layer].astype(BF16), w_down[layer].astype(BF16), g_final,
                          final_norm=(layer == DEPTH - 1))
    return x
```

```python
import functools

import jax
import jax.numpy as jnp
from jax import lax
from jax.experimental import pallas as pl
from jax.experimental.pallas import tpu as pltpu

D_MODEL = 1024
DEPTH = 2
HEAD_DIM = 64
ATT_WIDTH = 384
RET_WIDTH = 384
HGRN_WIDTH = 256
IN_WIDTH = 3 * ATT_WIDTH + 4 * RET_WIDTH + 5 * HGRN_WIDTH
DILATED_CONFIGS = ((128, 1), (512, 4), (2048, 16))
N_SIDE = 64
RET_CHUNK = 128
HGRN_CHUNK = 16
FFN_HIDDEN = 2816
FFN_CHUNKS = (512, 512, 512, 512, 512, 256)
N_MOD = 6
NORM_EPS = 1e-6
NEG_INF = -1e30
LB_FLOOR = 1e-30

LANES = 128
SUBLANES = 8
LOG2E = 1.4426950408889634
ATT_Q_BLK, ATT_K_BLK, ATT_V_BLK = 0, 3, 6
RET_Q_BLK, RET_K_BLK, RET_V_BLK, RET_G_BLK = 9, 12, 15, 18
HG_Q_BLK, HG_ZF_BLK, HG_ZB_BLK, HG_I_BLK, HG_G_BLK = 21, 23, 25, 27, 29

VMEM_LIMIT = 56 * 1024 * 1024

F32 = jnp.float32
BF16 = jnp.bfloat16

NT_DIMS = (((1,), (1,)), ((), ()))
TN_DIMS = (((0,), (0,)), ((), ()))


def _dot(a, b, dims=None):
    if dims is None:
        return jnp.dot(a, b, preferred_element_type=F32)
    return lax.dot_general(a, b, dims, preferred_element_type=F32)


def _split2(x):
    hi = x.astype(BF16)
    lo = (x - hi.astype(F32)).astype(BF16)
    return hi, lo


def _split3(x):
    hi = x.astype(BF16)
    r1 = x - hi.astype(F32)
    mid = r1.astype(BF16)
    lo = (r1 - mid.astype(F32)).astype(BF16)
    return hi, mid, lo


def _head_block_mask(shape):
    r = lax.broadcasted_iota(jnp.int32, shape, 0)
    c = lax.broadcasted_iota(jnp.int32, shape, 1)
    return (r >= HEAD_DIM) == (c >= HEAD_DIM)


def _head_sum_matrix():
    return jnp.where(_head_block_mask((LANES, LANES)), 1.0, 0.0).astype(BF16)


def _sigmoid(x):
    return 1.0 / (1.0 + jnp.exp(-x))


def _mod_kernel(c_ref, w_ref, b_ref, o_ref):
    c = c_ref[...]
    cond = c * _sigmoid(c)
    o_ref[0] = jnp.dot(cond, w_ref[0], preferred_element_type=F32,
                       precision=lax.Precision.HIGHEST) + b_ref[0]


def _modulation(c, w_ada, b_ada):
    B, D = c.shape
    n_out = w_ada.shape[-1]
    tn = 1536
    return pl.pallas_call(
        _mod_kernel,
        out_shape=jax.ShapeDtypeStruct((DEPTH, B, n_out), F32),
        grid=(DEPTH, n_out // tn),
        in_specs=[
            pl.BlockSpec((B, D), lambda l, j: (0, 0)),
            pl.BlockSpec((1, D, tn), lambda l, j: (l, 0, j)),
            pl.BlockSpec((1, 1, tn), lambda l, j: (l, 0, j)),
        ],
        out_specs=pl.BlockSpec((1, B, tn), lambda l, j: (l, 0, j)),
        compiler_params=pltpu.CompilerParams(
            dimension_semantics=("arbitrary", "arbitrary"), vmem_limit_bytes=VMEM_LIMIT),
        name="adaln_mod",
    )(c, w_ada, b_ada.reshape(DEPTH, 1, n_out))


def _inproj_kernel(x_ref, g_ref, sc_ref, sh_ref, w_ref, o_ref):
    x = x_ref[0]
    y = x * lax.rsqrt(jnp.mean(x * x, axis=-1, keepdims=True) + NORM_EPS) * g_ref[...]
    h = y * (1.0 + sc_ref[0]) + sh_ref[0]
    o_ref[0] = _dot(h.astype(BF16), w_ref[...])


def _in_projection(x, gain, scale, shift, w_bf16):
    B, S, D = x.shape
    N = w_bf16.shape[-1]
    tm = 512
    return pl.pallas_call(
        _inproj_kernel,
        out_shape=jax.ShapeDtypeStruct((B, S, N), F32),
        grid=(B, S // tm),
        in_specs=[
            pl.BlockSpec((1, tm, D), lambda b, i: (b, i, 0)),
            pl.BlockSpec((1, D), lambda b, i: (0, 0)),
            pl.BlockSpec((1, 1, D), lambda b, i: (b, 0, 0)),
            pl.BlockSpec((1, 1, D), lambda b, i: (b, 0, 0)),
            pl.BlockSpec((D, N), lambda b, i: (0, 0), pipeline_mode=pl.Buffered(1)),
        ],
        out_specs=pl.BlockSpec((1, tm, N), lambda b, i: (b, i, 0)),
        compiler_params=pltpu.CompilerParams(
            dimension_semantics=("arbitrary", "arbitrary"), vmem_limit_bytes=VMEM_LIMIT),
        name="in_proj",
    )(x, gain.reshape(1, D), scale, shift, w_bf16)


ATT_KEYS = 3 * N_SIDE
ATT_UNROLL = 4
RET_UNROLL = 8


def _att_kernel(q_ref, k_ref, v_ref, o_ref, bias_ref, qd_ref, kd_ref, vd_ref, t4_ref, ob_ref, lse_ref,
                s_ref, p_ref, rl_ref, ml_ref):
    S = q_ref.shape[1]
    pair = pl.program_id(1)
    lane = lax.broadcasted_iota(jnp.int32, (1, LANES), 1)
    in_head = (lane < HEAD_DIM, lane >= HEAD_DIM)
    lane_q = lax.broadcasted_iota(jnp.int32, (N_SIDE, LANES), 1)
    q_in_head = (lane_q < HEAD_DIM, lane_q >= HEAD_DIM)

    qi = lax.broadcasted_iota(jnp.int32, (N_SIDE, ATT_KEYS), 0)
    kj = lax.broadcasted_iota(jnp.int32, (N_SIDE, ATT_KEYS), 1)
    for br, (_, dil) in enumerate(DILATED_CONFIGS):
        for case in range(3):
            steps = jnp.abs(qi + N_SIDE * case - kj)
            dist = (dil * steps).astype(F32)
            for h in range(2):
                head = (2 * pair + h + 1).astype(F32) + jnp.zeros((1, ATT_KEYS), F32)
                slope = jnp.exp2(-8.0 * head / (ATT_WIDTH // HEAD_DIM)) * LOG2E
                bias_ref[br, case, h * N_SIDE:(h + 1) * N_SIDE, :] = jnp.where(
                    steps <= N_SIDE, -slope * dist, NEG_INF)

    assert [d for _, d in DILATED_CONFIGS] == [1, 4, 16]
    piece = 256
    quarter = S // 4
    for src_ref, dst_ref, scale in ((q_ref, qd_ref, HEAD_DIM ** -0.5 * LOG2E), (k_ref, kd_ref, 1.0),
                                    (v_ref, vd_ref, 1.0)):
        def pass1(j, carry, src_ref=src_ref, dst_ref=dst_ref, scale=scale):
            dst = pl.ds(pl.multiple_of(j * piece, piece), piece)
            dst_ref[0, dst, :] = (src_ref[0, dst, :] * scale).astype(BF16)
            cls = j // (quarter // piece)
            first = (j % (quarter // piece)) * piece
            x = src_ref[0, pl.ds(cls + 4 * first, piece, stride=4), :] * scale
            t4_ref[dst, :] = x
            dst_ref[1, dst, :] = x.astype(BF16)
            return carry

        lax.fori_loop(0, S // piece, pass1, 0)

        def pass2(j, carry, dst_ref=dst_ref):
            dst = pl.ds(pl.multiple_of(j * piece, piece), piece)
            x = t4_ref[pl.ds((j % 4) * quarter + j // 4, piece, stride=4), :]
            dst_ref[2, dst, :] = x.astype(BF16)
            return carry

        lax.fori_loop(0, S // piece, pass2, 0)

    zero_q = jnp.zeros((N_SIDE, LANES), BF16)
    for br, (_, dil) in enumerate(DILATED_CONFIGS):
        sub_len = S // dil
        n_blocks = sub_len // N_SIDE
        blk_bits = n_blocks.bit_length() - 1

        n_groups = dil * n_blocks // ATT_UNROLL

        def block_index(g, u, n_blocks=n_blocks, blk_bits=blk_bits, sub_len=sub_len):
            idx = g * ATT_UNROLL + u
            c = jnp.right_shift(idx, blk_bits)
            n = jnp.bitwise_and(idx, n_blocks - 1)
            win = jnp.clip(N_SIDE * n - N_SIDE, 0, sub_len - ATT_KEYS)
            case = jnp.right_shift(N_SIDE * n - win, 6)
            q_src = pl.ds(pl.multiple_of(N_SIDE * idx, N_SIDE), N_SIDE)
            k_src = pl.ds(pl.multiple_of(N_SIDE * (idx - n) + win, N_SIDE), ATT_KEYS)
            return c, n, case, q_src, k_src

        def scores_stage(g, slot, br=br):
            for u in range(ATT_UNROLL):
                _, _, case, q_src, k_src = block_index(g, u)
                q = qd_ref[br, q_src, :]
                q2 = jnp.concatenate([jnp.where(q_in_head[h], q, zero_q) for h in range(2)], axis=0)
                s_ref[slot, u] = _dot(q2, kd_ref[br, k_src, :], NT_DIMS) + bias_ref[br, case]

        def softmax_stage(slot):
            for u in range(ATT_UNROLL):
                s = s_ref[slot, u]
                m = jnp.max(s, axis=-1, keepdims=True)
                p = jnp.exp2(s - m)
                l = jnp.sum(p, axis=-1, keepdims=True)
                p_ref[slot, u] = p.astype(BF16)
                rl_ref[slot, u] = jnp.broadcast_to(1.0 / l, (2 * N_SIDE, LANES))
                ml_ref[slot, u] = jnp.broadcast_to(m + jnp.log2(l), (2 * N_SIDE, LANES))

        def values_stage(g, slot, br=br, dil=dil):
            for u in range(ATT_UNROLL):
                c, n, _, q_src, k_src = block_index(g, u)
                rows = q_src if dil == 1 else pl.ds(c + dil * N_SIDE * n, N_SIDE, stride=dil)
                o = _dot(p_ref[slot, u], vd_ref[br, k_src, :]) * rl_ref[slot, u]
                ml = ml_ref[slot, u]
                ob_ref[br, rows, :] = jnp.where(in_head[0], o[:N_SIDE], o[N_SIDE:])
                lse_ref[br, rows, :] = jnp.where(in_head[0], ml[:N_SIDE], ml[N_SIDE:])

        def trip(i, carry):
            for par in range(2):
                g = 2 * i + par
                values_stage(g - 2, par)
                softmax_stage(1 - par)
                scores_stage(g, par)
            return carry

        scores_stage(0, 0)
        softmax_stage(0)
        scores_stage(1, 1)
        lax.fori_loop(1, n_groups // 2, trip, 0)
        values_stage(n_groups - 2, 0)
        softmax_stage(1)
        values_stage(n_groups - 1, 1)

    tile = 512

    def merge_body(t, carry):
        rows = pl.ds(pl.multiple_of(t * tile, tile), tile)
        l0, l1, l2 = lse_ref[0, rows, :], lse_ref[1, rows, :], lse_ref[2, rows, :]
        m = jnp.maximum(jnp.maximum(l0, l1), l2)
        e0, e1, e2 = jnp.exp2(l0 - m), jnp.exp2(l1 - m), jnp.exp2(l2 - m)
        num = e0 * ob_ref[0, rows, :] + e1 * ob_ref[1, rows, :] + e2 * ob_ref[2, rows, :]
        o_ref[0, rows, :] = num / (e0 + e1 + e2)
        return carry

    lax.fori_loop(0, S // tile, merge_body, 0)


def _attention(proj):
    B, S, _ = proj.shape
    n_pairs = ATT_WIDTH // LANES
    n_br = len(DILATED_CONFIGS)

    def col(blk):
        return pl.BlockSpec((1, S, LANES), lambda b, p: (b, 0, blk + p))

    return pl.pallas_call(
        _att_kernel,
        out_shape=jax.ShapeDtypeStruct((B, S, ATT_WIDTH), F32),
        grid=(B, n_pairs),
        in_specs=[col(ATT_Q_BLK), col(ATT_K_BLK), col(ATT_V_BLK)],
        out_specs=pl.BlockSpec((1, S, LANES), lambda b, p: (b, 0, p)),
        scratch_shapes=[
            pltpu.VMEM((n_br, 3, 2 * N_SIDE, ATT_KEYS), F32),
            pltpu.VMEM((n_br, S, LANES), BF16),
            pltpu.VMEM((n_br, S, LANES), BF16),
            pltpu.VMEM((n_br, S, LANES), BF16),
            pltpu.VMEM((S, LANES), F32),
            pltpu.VMEM((n_br, S, LANES), F32),
            pltpu.VMEM((n_br, S, LANES), F32),
            pltpu.VMEM((2, ATT_UNROLL, 2 * N_SIDE, ATT_KEYS), F32),
            pltpu.VMEM((2, ATT_UNROLL, 2 * N_SIDE, ATT_KEYS), BF16),
            pltpu.VMEM((2, ATT_UNROLL, 2 * N_SIDE, LANES), F32),
            pltpu.VMEM((2, ATT_UNROLL, 2 * N_SIDE, LANES), F32),
        ],
        compiler_params=pltpu.CompilerParams(
            dimension_semantics=("arbitrary", "arbitrary"), vmem_limit_bytes=VMEM_LIMIT),
        name="dilated_attention",
    )(proj, proj, proj)


def _ret_kernel(q_ref, k_ref, v_ref, g_ref, gn_ref, o_ref, sf_ref, sb_ref):
    S = q_ref.shape[1]
    C = RET_CHUNK
    n_chunks = S // C
    pair = pl.program_id(1)
    lane = lax.broadcasted_iota(jnp.int32, (1, LANES), 1)
    in_head = (lane < HEAD_DIM, lane >= HEAD_DIM)
    block = _head_block_mask((LANES, LANES))
    hsum = _head_sum_matrix()

    def log_gamma(head_f32):
        return jnp.log1p(-jnp.exp2(-5.0 - head_f32))

    zeros_row = jnp.zeros((1, LANES), F32)
    lg_head = [log_gamma((2 * pair + h).astype(F32) + zeros_row) for h in range(2)]
    lg = jnp.where(in_head[0], lg_head[0], lg_head[1])
    pos = lax.broadcasted_iota(jnp.int32, (C, LANES), 0).astype(F32)
    k_fwd = jnp.exp(lg * (C - 1.0 - pos))
    k_bwd = jnp.exp(lg * pos)
    q_fwd = jnp.exp(lg * (pos + 1.0))
    q_bwd = jnp.exp(lg * (C - pos))
    state_decay = jnp.where(block, jnp.exp(lg * C), 0.0)
    ri = lax.broadcasted_iota(jnp.int32, (C, C), 0)
    ci = lax.broadcasted_iota(jnp.int32, (C, C), 1)
    dist = jnp.abs(ri - ci).astype(F32)
    intra = [jnp.exp(lg_head[h] * dist) for h in range(2)]

    def chunk_rows(n):
        return pl.ds(pl.multiple_of(n * C, C), C)

    def incr_body(it, carry):
        for u in range(RET_UNROLL):
            n = it * RET_UNROLL + u
            rows = chunk_rows(n)
            k = k_ref[0, rows, :] * (HEAD_DIM ** -0.5)
            v = v_ref[0, rows, :]
            sf_ref[n] = jnp.where(block, _dot(k * k_fwd, v, TN_DIMS), 0.0)
            sb_ref[n] = jnp.where(block, _dot(k * k_bwd, v, TN_DIMS), 0.0)
        return carry

    lax.fori_loop(0, n_chunks // RET_UNROLL, incr_body, 0)

    def scan_fwd(n, state):
        incr = sf_ref[n]
        sf_ref[n] = state
        return state_decay * state + incr

    def scan_bwd(i, state):
        n = n_chunks - 1 - i
        incr = sb_ref[n]
        sb_ref[n] = state
        return state_decay * state + incr

    lax.fori_loop(0, n_chunks, scan_fwd, jnp.zeros((LANES, LANES), F32), unroll=4)
    lax.fori_loop(0, n_chunks, scan_bwd, jnp.zeros((LANES, LANES), F32), unroll=4)

    gn = gn_ref[...]

    def out_body(it, carry):
        rows = [chunk_rows(it * RET_UNROLL + u) for u in range(RET_UNROLL)]
        qs = [q_ref[0, r, :] for r in rows]
        vs = [v_ref[0, r, :] for r in rows]
        scores = []
        for u in range(RET_UNROLL):
            k = k_ref[0, rows[u], :] * (HEAD_DIM ** -0.5)
            scores.append([_dot(jnp.where(in_head[h], qs[u], 0.0), k, NT_DIMS) for h in range(2)])
        outs = []
        for u in range(RET_UNROLL):
            n = it * RET_UNROLL + u
            lhs = jnp.concatenate([(qs[u] * q_fwd).astype(BF16), (qs[u] * q_bwd).astype(BF16)], axis=1)
            rhs = jnp.concatenate([sf_ref[n].astype(BF16), sb_ref[n].astype(BF16)], axis=0)
            outs.append(_dot(lhs, rhs))
        for u in range(RET_UNROLL):
            lhs = jnp.concatenate([(scores[u][h] * intra[h]).astype(BF16) for h in range(2)], axis=1)
            rhs = jnp.concatenate([jnp.where(in_head[h], vs[u], 0.0).astype(BF16) for h in range(2)], axis=0)
            outs[u] = outs[u] + _dot(lhs, rhs)
        hsum2 = jnp.concatenate([hsum, hsum], axis=0)
        mus = []
        for u in range(RET_UNROLL):
            mus.append(_dot(jnp.concatenate(_split2(outs[u]), axis=1), hsum2) * (1.0 / HEAD_DIM))
        cens = [outs[u] - mus[u] for u in range(RET_UNROLL)]
        for u in range(RET_UNROLL):
            var = _dot(jnp.concatenate(_split2(cens[u] * cens[u]), axis=1), hsum2) * (1.0 / HEAD_DIM)
            g = g_ref[0, rows[u], :]
            o_ref[0, rows[u], :] = cens[u] * lax.rsqrt(var + NORM_EPS) * gn * (g * _sigmoid(g))
        return carry

    lax.fori_loop(0, n_chunks // RET_UNROLL, out_body, 0)


def _retention(proj, ret_gn):
    B, S, _ = proj.shape
    n_pairs = RET_WIDTH // LANES

    def col(blk):
        return pl.BlockSpec((1, S, LANES), lambda b, p: (b, 0, blk + p))

    return pl.pallas_call(
        _ret_kernel,
        out_shape=jax.ShapeDtypeStruct((B, S, RET_WIDTH), F32),
        grid=(B, n_pairs),
        in_specs=[col(RET_Q_BLK), col(RET_K_BLK), col(RET_V_BLK), col(RET_G_BLK),
                  pl.BlockSpec((1, LANES), lambda b, p: (0, p))],
        out_specs=pl.BlockSpec((1, S, LANES), lambda b, p: (b, 0, p)),
        scratch_shapes=[pltpu.VMEM((S // RET_CHUNK, LANES, LANES), F32),
                        pltpu.VMEM((S // RET_CHUNK, LANES, LANES), F32)],
        compiler_params=pltpu.CompilerParams(
            dimension_semantics=("arbitrary", "arbitrary"), vmem_limit_bytes=VMEM_LIMIT),
        name="retention",
    )(proj, proj, proj, proj, ret_gn.reshape(1, RET_WIDTH))


HGRN_TILE = 128
HGRN_SLABS = HGRN_TILE * (HGRN_CHUNK // SUBLANES) * 3 // 4
HGRN_SLOTS = 4


def _hgrn_kernel(layer, q_ref, zf_ref, zb_ref, i_ref, lbl_ref, o_ref,
                 qt_s, b_s, c_s, v_s, dec_s, qe_s, ke_s, w_s, mask_s):
    S = q_ref.shape[1]
    T, C = HGRN_TILE, HGRN_CHUNK
    n_tiles = S // T
    chunks = T // C
    block = _head_block_mask((LANES, LANES))
    hsum = _head_sum_matrix()

    logits = [lbl_ref[l:l + 1, :] for l in range(DEPTH)]
    mx = functools.reduce(jnp.maximum, logits)
    ex = [jnp.exp(t - mx) for t in logits]
    den = functools.reduce(lambda a, b: a + b, ex)
    probs = [e / den for e in ex]
    cum = functools.reduce(lambda a, b: a + b, probs[:layer + 1])
    lb = jnp.clip(cum - probs[0], 0.0, 1.0 - 1e-6)
    lb_floor = jnp.maximum(lb, LB_FLOOR)
    one_minus_lb = 1.0 - lb

    ri = lax.broadcasted_iota(jnp.int32, (T, T), 0)
    ci = lax.broadcasted_iota(jnp.int32, (T, T), 1)
    chunk_bits = C.bit_length() - 1
    same_chunk = jnp.right_shift(ri, chunk_bits) == jnp.right_shift(ci, chunk_bits)
    tri = (jnp.where(same_chunk & (ci <= ri), 1.0, 0.0).astype(BF16),
           jnp.where(same_chunk & (ci >= ri), 1.0, 0.0).astype(BF16))
    ones_chunk = jnp.where(same_chunk, 1.0, 0.0).astype(BF16)

    def exact_sum(mat, parts):
        return _dot(mat, parts[0]) + _dot(mat, parts[1]) + _dot(mat, parts[2])

    t_loc = lax.broadcasted_iota(jnp.int32, (SUBLANES, LANES), 0)
    for s in range(SUBLANES):
        mask_s[0, s] = jnp.where(t_loc >= s, 0.0, NEG_INF)
        mask_s[1, s] = jnp.where(t_loc <= s, 0.0, NEG_INF)

    def slab_list(backward):
        slabs = []
        for c in range(chunks):
            for s in range(C):
                for half in range(C // SUBLANES):
                    if (half <= s // SUBLANES) if backward else (half >= s // SUBLANES):
                        slabs.append((c, s, half))
        return slabs

    slabs_of = {False: slab_list(False), True: slab_list(True)}

    def tile_rows(step, backward):
        tile = (n_tiles - 1 - step) if backward else step
        if isinstance(tile, int):
            return pl.ds(tile * T, T)
        return pl.ds(pl.multiple_of(tile * T, T), T)

    def prepare(step, backward, slot):
        rows = tile_rows(step, backward)
        z = (zb_ref if backward else zf_ref)[0, rows, :]
        hq = q_ref[0, rows, :]
        q = hq * _sigmoid(hq)
        e = jnp.exp(-jnp.abs(z))
        r = 1.0 / (1.0 + e)
        er = e * r
        nonneg = z >= 0.0
        f = lb_floor + one_minus_lb * jnp.where(nonneg, r, er)
        kk = one_minus_lb * jnp.where(nonneg, er, r)
        parts = _split3(jnp.log(f))
        b2 = exact_sum(tri[1] if backward else tri[0], parts) * LOG2E
        b2_tot = exact_sum(ones_chunk, parts) * LOG2E
        qt_s[slot] = q
        b_s[slot] = b2
        c_s[slot] = b2 - jnp.log(kk) * LOG2E
        v_s[slot] = i_ref[0, rows, :]
        qe_s[slot] = (q * jnp.exp2(b2)).astype(BF16)
        ke_s[slot] = (kk * jnp.exp2(b2_tot - b2)).astype(BF16)
        dec_s[slot] = jnp.exp2(b2_tot)

    def advance(backward, slot, state):
        slabs = slabs_of[backward]
        per_chunk = len(slabs) // chunks
        attns = []
        for c in range(chunks):
            base = c * per_chunk
            for i in range(base, base + per_chunk, 2):
                ws = []
                for cc, s, half in slabs[i:i + 2]:
                    lo = cc * C + half * SUBLANES
                    c_src = jnp.broadcast_to(c_s[slot, cc * C + s:cc * C + s + 1, :], (SUBLANES, LANES))
                    expo = b_s[slot, lo:lo + SUBLANES, :] - c_src
                    if half == s // SUBLANES:
                        expo = expo + mask_s[int(backward), s % SUBLANES]
                    ws.append(qt_s[slot, lo:lo + SUBLANES, :] * jnp.exp2(expo))
                w_s[slot, i * SUBLANES:(i + 2) * SUBLANES, :] = jnp.concatenate(ws, axis=0).astype(BF16)
            attns.append(_dot(w_s[slot, base * SUBLANES:(base + per_chunk) * SUBLANES, :], hsum))

        order = range(chunks - 1, -1, -1) if backward else range(chunks)
        upds = [_dot(v_s[slot, c * C:(c + 1) * C, :].astype(BF16), ke_s[slot, c * C:(c + 1) * C, :], TN_DIMS)
                for c in range(chunks)]
        states = [None] * chunks
        for c in order:
            states[c] = state
            state = state * dec_s[slot, c * C:c * C + 1, :] + jnp.where(block, upds[c], 0.0)
        outs = [_dot(qe_s[slot, c * C:(c + 1) * C, :], states[c].astype(BF16), NT_DIMS) for c in range(chunks)]

        pieces = []
        for c in range(chunks):
            acc = [None] * (C // SUBLANES)
            for i in range(per_chunk):
                cc, s, half = slabs[c * per_chunk + i]
                v_src = jnp.broadcast_to(v_s[slot, cc * C + s:cc * C + s + 1, :], (SUBLANES, LANES))
                term = attns[c][i * SUBLANES:(i + 1) * SUBLANES] * v_src
                acc[half] = term if acc[half] is None else acc[half] + term
            pieces.extend(acc[h] + outs[c][h * SUBLANES:(h + 1) * SUBLANES] for h in range(C // SUBLANES))
        return jnp.concatenate(pieces, axis=0), state

    for backward in (False, True):
        def trip(i, state, backward=backward):
            base = 4 * i
            outs = []
            prepare(base + 2, backward, 2)
            prepare(base + 3, backward, 3)
            for slot in (0, 1):
                out, state = advance(backward, slot, state)
                outs.append(out)
            prepare(jnp.minimum(base + 4, n_tiles - 1), backward, 0)
            prepare(jnp.minimum(base + 5, n_tiles - 1), backward, 1)
            for slot in (2, 3):
                out, state = advance(backward, slot, state)
                outs.append(out)
            for k, out in enumerate(outs):
                rows = tile_rows(base + k, backward)
                o_ref[0, rows, :] = (o_ref[0, rows, :] + out) if backward else out
            return state

        prepare(0, backward, 0)
        prepare(1, backward, 1)
        lax.fori_loop(0, n_tiles // 4, trip, jnp.zeros((LANES, LANES), F32))


def _hgrn(proj, lb_logits, layer):
    B, S, _ = proj.shape
    n_pairs = HGRN_WIDTH // LANES

    def col(blk):
        return pl.BlockSpec((1, S, LANES), lambda b, p: (b, 0, blk + p))

    return pl.pallas_call(
        functools.partial(_hgrn_kernel, layer),
        out_shape=jax.ShapeDtypeStruct((B, S, HGRN_WIDTH), F32),
        grid=(B, n_pairs),
        in_specs=[col(HG_Q_BLK), col(HG_ZF_BLK), col(HG_ZB_BLK), col(HG_I_BLK),
                  pl.BlockSpec((DEPTH, LANES), lambda b, p: (0, p))],
        out_specs=pl.BlockSpec((1, S, LANES), lambda b, p: (b, 0, p)),
        scratch_shapes=(
            [pltpu.VMEM((HGRN_SLOTS, HGRN_TILE, LANES), F32)] * 5
            + [pltpu.VMEM((HGRN_SLOTS, HGRN_TILE, LANES), BF16)] * 2
            + [pltpu.VMEM((HGRN_SLOTS, HGRN_SLABS * SUBLANES, LANES), BF16)]
            + [pltpu.VMEM((2, SUBLANES, SUBLANES, LANES), F32)]),
        compiler_params=pltpu.CompilerParams(
            dimension_semantics=("arbitrary", "arbitrary"), vmem_limit_bytes=VMEM_LIMIT),
        name="hgrn2",
    )(proj, proj, proj, proj, lb_logits)


def _ffn_kernel(final_norm, att_ref, ret_ref, hg_ref, hga_ref, hgb_ref, x_ref, wo_ref, hgn_ref,
                gate1_ref, gffn_ref, sc2_ref, sh2_ref, gate2_ref, wgu_ref, wd_ref, gfin_ref,
                o_ref, x2_s, h_s, acc_s):
    hraw = hg_ref[0]
    hgate = jnp.concatenate([hga_ref[0], hgb_ref[0]], axis=-1)
    hgo = (hraw * lax.rsqrt(jnp.mean(hraw * hraw, axis=-1, keepdims=True) + NORM_EPS)
           * hgn_ref[...] * (hgate * _sigmoid(hgate)))
    y = (_dot(att_ref[0].astype(BF16), wo_ref[0:ATT_WIDTH, :])
         + _dot(ret_ref[0].astype(BF16), wo_ref[ATT_WIDTH:ATT_WIDTH + RET_WIDTH, :])
         + _dot(hgo.astype(BF16), wo_ref[ATT_WIDTH + RET_WIDTH:, :]))
    x2 = x_ref[0] + gate1_ref[0] * y
    x2_s[...] = x2
    n = x2 * lax.rsqrt(jnp.mean(x2 * x2, axis=-1, keepdims=True) + NORM_EPS) * gffn_ref[...]
    h_s[...] = (n * (1.0 + sc2_ref[0]) + sh2_ref[0]).astype(BF16)

    lo = 0
    for width in FFN_CHUNKS:
        h = h_s[...]
        g = _dot(h, wgu_ref[:, lo:lo + width])
        u = _dot(h, wgu_ref[:, FFN_HIDDEN + lo:FFN_HIDDEN + lo + width])
        a = (g * _sigmoid(g) * u).astype(BF16)
        down = _dot(a, wd_ref[lo:lo + width, :])
        acc_s[...] = down if lo == 0 else acc_s[...] + down
        lo += width

    out = x2_s[...] + gate2_ref[0] * acc_s[...]
    if final_norm:
        out = out * lax.rsqrt(jnp.mean(out * out, axis=-1, keepdims=True) + NORM_EPS) * gfin_ref[...]
    o_ref[0] = out


def _out_proj_ffn(att, ret, hg_raw, proj, x, wo, hgrn_gn, gate1, g_ffn, scale2, shift2, gate2,
                  w_gate_up, w_down, g_final, final_norm):
    B, S, D = x.shape
    tm = 512

    def rows(width):
        return pl.BlockSpec((1, tm, width), lambda b, i: (b, i, 0))

    def per_batch():
        return pl.BlockSpec((1, 1, D), lambda b, i: (b, 0, 0))

    def vec(width):
        return pl.BlockSpec((1, width), lambda b, i: (0, 0))

    def resident(shape):
        return pl.BlockSpec(shape, lambda b, i: (0, 0), pipeline_mode=pl.Buffered(1))

    return pl.pallas_call(
        functools.partial(_ffn_kernel, final_norm),
        out_shape=jax.ShapeDtypeStruct((B, S, D), F32),
        grid=(B, S // tm),
        in_specs=[
            rows(ATT_WIDTH), rows(RET_WIDTH), rows(HGRN_WIDTH),
            pl.BlockSpec((1, tm, LANES), lambda b, i: (b, i, HG_G_BLK)),
            pl.BlockSpec((1, tm, LANES), lambda b, i: (b, i, HG_G_BLK + 1)),
            rows(D),
            resident((D, D)),
            vec(HGRN_WIDTH),
            per_batch(), vec(D), per_batch(), per_batch(), per_batch(),
            resident((D, 2 * FFN_HIDDEN)),
            resident((FFN_HIDDEN, D)),
            vec(D),
        ],
        out_specs=pl.BlockSpec((1, tm, D), lambda b, i: (b, i, 0)),
        scratch_shapes=[
            pltpu.VMEM((tm, D), F32),
            pltpu.VMEM((tm, D), BF16),
            pltpu.VMEM((tm, D), F32),
        ],
        compiler_params=pltpu.CompilerParams(
            dimension_semantics=("arbitrary", "arbitrary"), vmem_limit_bytes=VMEM_LIMIT),
        name="out_proj_ffn",
    )(att, ret, hg_raw, proj, proj, x, wo, hgrn_gn.reshape(1, HGRN_WIDTH), gate1,
      g_ffn.reshape(1, D), scale2, shift2, gate2, w_gate_up, w_down, g_final.reshape(1, D))


def kernel(x, c, w_ada, b_ada, g_mix, w_in, ret_gn, hgrn_gn, hgrn_lb_logits, w_out,
           g_ffn, w_gate_up, w_down, g_final):
    B, S, D = x.shape
    mod = _modulation(c, w_ada, b_ada).reshape(DEPTH, B, N_MOD, 1, D)
    for layer in range(DEPTH):
        shift1, scale1, gate1, shift2, scale2, gate2 = (mod[layer, :, i] for i in range(N_MOD))
        proj = _in_projection(x, g_mix[layer], scale1, shift1, w_in[layer].astype(BF16))
        att = _attention(proj)
        ret = _retention(proj, ret_gn[layer])
        hg_raw = _hgrn(proj, hgrn_lb_logits, layer)
        x = _out_proj_ffn(att, ret, hg_raw, proj, x, w_out[layer].astype(BF16), hgrn_gn[layer],
                          gate1, g_ffn[layer], scale2, shift2, gate2,
                          w_gate_up[layer].astype(BF16), w_down[layer].astype(BF16), g_final,
                          final_norm=(layer == DEPTH - 1))
    return x
```

```python
import functools

import jax
import jax.numpy as jnp
from jax import lax
from jax.experimental import pallas as pl
from jax.experimental.pallas import tpu as pltpu

D_MODEL = 1024
DEPTH = 2
HEAD_DIM = 64
ATT_WIDTH = 384
RET_WIDTH = 384
HGRN_WIDTH = 256
IN_WIDTH = 3 * ATT_WIDTH + 4 * RET_WIDTH + 5 * HGRN_WIDTH
DILATED_CONFIGS = ((128, 1), (512, 4), (2048, 16))
N_SIDE = 64
RET_CHUNK = 128
HGRN_CHUNK = 16
FFN_HIDDEN = 2816
FFN_CHUNKS = (512, 512, 512, 512, 512, 256)
N_MOD = 6
NORM_EPS = 1e-6
NEG_INF = -1e30
LB_FLOOR = 1e-30

LANES = 128
SUBLANES = 8
LOG2E = 1.4426950408889634
ATT_Q_BLK, ATT_K_BLK, ATT_V_BLK = 0, 3, 6
RET_Q_BLK, RET_K_BLK, RET_V_BLK, RET_G_BLK = 9, 12, 15, 18
HG_Q_BLK, HG_ZF_BLK, HG_ZB_BLK, HG_I_BLK, HG_G_BLK = 21, 23, 25, 27, 29

VMEM_LIMIT = 56 * 1024 * 1024

F32 = jnp.float32
BF16 = jnp.bfloat16

NT_DIMS = (((1,), (1,)), ((), ()))
TN_DIMS = (((0,), (0,)), ((), ()))


def _dot(a, b, dims=None):
    if dims is None:
        return jnp.dot(a, b, preferred_element_type=F32)
    return lax.dot_general(a, b, dims, preferred_element_type=F32)


def _split2(x):
    hi = x.astype(BF16)
    lo = (x - hi.astype(F32)).astype(BF16)
    return hi, lo


def _split3(x):
    hi = x.astype(BF16)
    r1 = x - hi.astype(F32)
    mid = r1.astype(BF16)
    lo = (r1 - mid.astype(F32)).astype(BF16)
    return hi, mid, lo


def _head_block_mask(shape):
    r = lax.broadcasted_iota(jnp.int32, shape, 0)
    c = lax.broadcasted_iota(jnp.int32, shape, 1)
    return (r >= HEAD_DIM) == (c >= HEAD_DIM)


def _head_sum_matrix():
    return jnp.where(_head_block_mask((LANES, LANES)), 1.0, 0.0).astype(BF16)


def _sigmoid(x):
    return 1.0 / (1.0 + jnp.exp(-x))


def _mod_kernel(c_ref, w_ref, b_ref, o_ref):
    c = c_ref[...]
    cond = c * _sigmoid(c)
    o_ref[0] = jnp.dot(cond, w_ref[0], preferred_element_type=F32,
                       precision=lax.Precision.HIGHEST) + b_ref[0]


def _modulation(c, w_ada, b_ada):
    B, D = c.shape
    n_out = w_ada.shape[-1]
    tn = 1536
    return pl.pallas_call(
        _mod_kernel,
        out_shape=jax.ShapeDtypeStruct((DEPTH, B, n_out), F32),
        grid=(DEPTH, n_out // tn),
        in_specs=[
            pl.BlockSpec((B, D), lambda l, j: (0, 0)),
            pl.BlockSpec((1, D, tn), lambda l, j: (l, 0, j)),
            pl.BlockSpec((1, 1, tn), lambda l, j: (l, 0, j)),
        ],
        out_specs=pl.BlockSpec((1, B, tn), lambda l, j: (l, 0, j)),
        compiler_params=pltpu.CompilerParams(
            dimension_semantics=("arbitrary", "arbitrary"), vmem_limit_bytes=VMEM_LIMIT),
        name="adaln_mod",
    )(c, w_ada, b_ada.reshape(DEPTH, 1, n_out))


def _inproj_kernel(x_ref, g_ref, sc_ref, sh_ref, w_ref, o_ref):
    x = x_ref[0]
    y = x * lax.rsqrt(jnp.mean(x * x, axis=-1, keepdims=True) + NORM_EPS) * g_ref[...]
    h = y * (1.0 + sc_ref[0]) + sh_ref[0]
    o_ref[0] = _dot(h.astype(BF16), w_ref[...])


def _in_projection(x, gain, scale, shift, w_bf16):
    B, S, D = x.shape
    N = w_bf16.shape[-1]
    tm = 512
    return pl.pallas_call(
        _inproj_kernel,
        out_shape=jax.ShapeDtypeStruct((B, S, N), F32),
        grid=(B, S // tm),
        in_specs=[
            pl.BlockSpec((1, tm, D), lambda b, i: (b, i, 0)),
            pl.BlockSpec((1, D), lambda b, i: (0, 0)),
            pl.BlockSpec((1, 1, D), lambda b, i: (b, 0, 0)),
            pl.BlockSpec((1, 1, D), lambda b, i: (b, 0, 0)),
            pl.BlockSpec((D, N), lambda b, i: (0, 0), pipeline_mode=pl.Buffered(1)),
        ],
        out_specs=pl.BlockSpec((1, tm, N), lambda b, i: (b, i, 0)),
        compiler_params=pltpu.CompilerParams(
            dimension_semantics=("arbitrary", "arbitrary"), vmem_limit_bytes=VMEM_LIMIT),
        name="in_proj",
    )(x, gain.reshape(1, D), scale, shift, w_bf16)


ATT_KEYS = 3 * N_SIDE
ATT_UNROLL = 8
RET_UNROLL = 8


def _att_kernel(q_ref, k_ref, v_ref, o_ref, bias_ref, qd_ref, kd_ref, vd_ref, t4_ref, ob_ref, mb_ref, lb_ref,
                s_ref, p_ref, m_ref, l_ref):
    S = q_ref.shape[1]
    pair = pl.program_id(1)
    lane = lax.broadcasted_iota(jnp.int32, (1, LANES), 1)
    in_head = (lane < HEAD_DIM, lane >= HEAD_DIM)
    lane_q = lax.broadcasted_iota(jnp.int32, (N_SIDE, LANES), 1)
    q_in_head = (lane_q < HEAD_DIM, lane_q >= HEAD_DIM)

    qi = lax.broadcasted_iota(jnp.int32, (N_SIDE, ATT_KEYS), 0)
    kj = lax.broadcasted_iota(jnp.int32, (N_SIDE, ATT_KEYS), 1)
    for br, (_, dil) in enumerate(DILATED_CONFIGS):
        for case in range(3):
            steps = jnp.abs(qi + N_SIDE * case - kj)
            dist = (dil * steps).astype(F32)
            for h in range(2):
                head = (2 * pair + h + 1).astype(F32) + jnp.zeros((1, ATT_KEYS), F32)
                slope = jnp.exp2(-8.0 * head / (ATT_WIDTH // HEAD_DIM)) * LOG2E
                bias_ref[br, case, h * N_SIDE:(h + 1) * N_SIDE, :] = jnp.where(
                    steps <= N_SIDE, -slope * dist, NEG_INF)

    assert [d for _, d in DILATED_CONFIGS] == [1, 4, 16]
    piece = 256
    quarter = S // 4
    for src_ref, dst_ref, scale in ((q_ref, qd_ref, HEAD_DIM ** -0.5 * LOG2E), (k_ref, kd_ref, 1.0),
                                    (v_ref, vd_ref, 1.0)):
        def pass1(j, carry, src_ref=src_ref, dst_ref=dst_ref, scale=scale):
            dst = pl.ds(pl.multiple_of(j * piece, piece), piece)
            dst_ref[0, dst, :] = (src_ref[0, dst, :] * scale).astype(BF16)
            cls = j // (quarter // piece)
            first = (j % (quarter // piece)) * piece
            x = src_ref[0, pl.ds(cls + 4 * first, piece, stride=4), :] * scale
            t4_ref[dst, :] = x
            dst_ref[1, dst, :] = x.astype(BF16)
            return carry

        lax.fori_loop(0, S // piece, pass1, 0)

        def pass2(j, carry, dst_ref=dst_ref):
            dst = pl.ds(pl.multiple_of(j * piece, piece), piece)
            x = t4_ref[pl.ds((j % 4) * quarter + j // 4, piece, stride=4), :]
            dst_ref[2, dst, :] = x.astype(BF16)
            return carry

        lax.fori_loop(0, S // piece, pass2, 0)

    zero_q = jnp.zeros((N_SIDE, LANES), BF16)
    branch_stages = []
    for br, (_, dil) in enumerate(DILATED_CONFIGS):
        sub_len = S // dil
        n_blocks = sub_len // N_SIDE
        blk_bits = n_blocks.bit_length() - 1

        n_groups = dil * n_blocks // ATT_UNROLL

        def block_index(g, u, n_blocks=n_blocks, blk_bits=blk_bits, sub_len=sub_len):
            idx = g * ATT_UNROLL + u
            c = jnp.right_shift(idx, blk_bits)
            n = jnp.bitwise_and(idx, n_blocks - 1)
            win = jnp.clip(N_SIDE * n - N_SIDE, 0, sub_len - ATT_KEYS)
            case = jnp.right_shift(N_SIDE * n - win, 6)
            q_src = pl.ds(pl.multiple_of(N_SIDE * idx, N_SIDE), N_SIDE)
            k_src = pl.ds(pl.multiple_of(N_SIDE * (idx - n) + win, N_SIDE), ATT_KEYS)
            return c, n, case, q_src, k_src

        def scores_stage(g, slot, br=br, block_index=block_index):
            for u in range(ATT_UNROLL):
                _, _, case, q_src, k_src = block_index(g, u)
                q = qd_ref[br, q_src, :]
                q2 = jnp.concatenate([jnp.where(q_in_head[h], q, zero_q) for h in range(2)], axis=0)
                s_ref[slot, u] = _dot(q2, kd_ref[br, k_src, :], NT_DIMS) + bias_ref[br, case]

        def values_stage(g, slot, br=br, dil=dil, block_index=block_index):
            for u in range(ATT_UNROLL):
                c, n, _, q_src, k_src = block_index(g, u)
                rows = q_src if dil == 1 else pl.ds(c + dil * N_SIDE * n, N_SIDE, stride=dil)
                o = _dot(p_ref[slot, u], vd_ref[br, k_src, :])
                m, l = m_ref[slot, u], l_ref[slot, u]
                ob_ref[br, rows, :] = jnp.where(in_head[0], o[:N_SIDE], o[N_SIDE:])
                mb_ref[br, rows, :] = jnp.where(in_head[0], m[:N_SIDE], m[N_SIDE:])
                lb_ref[br, rows, :] = jnp.where(in_head[0], l[:N_SIDE], l[N_SIDE:])

        branch_stages.append((scores_stage, values_stage, n_groups))

    def softmax_stage(slot):
        for u in range(ATT_UNROLL):
            s = s_ref[slot, u]
            m = jnp.max(s, axis=-1, keepdims=True)
            p = jnp.exp2(s - m)
            p_ref[slot, u] = p.astype(BF16)
            m_ref[slot, u] = jnp.broadcast_to(m, (2 * N_SIDE, LANES))
            l_ref[slot, u] = jnp.broadcast_to(jnp.sum(p, axis=-1, keepdims=True), (2 * N_SIDE, LANES))

    def tick(stage1, stage2_slot, stage3):
        if stage3 is not None:
            stage3[0](stage3[1], stage3[2])
        if stage2_slot is not None:
            softmax_stage(stage2_slot)
        if stage1 is not None:
            stage1[0](stage1[1], stage1[2])

    prev = None
    for scores_stage, values_stage, n_groups in branch_stages:
        assert n_groups % 2 == 0
        for par in range(2):
            older = None if prev is None else (prev[0], prev[1] - 2 + par, par)
            middle = (1 - par) if (prev is not None or par == 1) else None
            tick((scores_stage, par, par), middle, older)

        def trip(i, carry, scores_stage=scores_stage, values_stage=values_stage):
            for par in range(2):
                g = 2 * i + par
                tick((scores_stage, g, par), 1 - par, (values_stage, g - 2, par))
            return carry

        lax.fori_loop(1, n_groups // 2, trip, 0)
        prev = (values_stage, n_groups)
    tick(None, 1, (prev[0], prev[1] - 2, 0))
    tick(None, None, (prev[0], prev[1] - 1, 1))

    tile = 512

    def merge_body(t, carry):
        rows = pl.ds(pl.multiple_of(t * tile, tile), tile)
        ms = [mb_ref[b, rows, :] for b in range(len(DILATED_CONFIGS))]
        top = functools.reduce(jnp.maximum, ms)
        es = [jnp.exp2(m - top) for m in ms]
        num = functools.reduce(lambda a, b: a + b, [e * ob_ref[b, rows, :] for b, e in enumerate(es)])
        den = functools.reduce(lambda a, b: a + b, [e * lb_ref[b, rows, :] for b, e in enumerate(es)])
        o_ref[0, rows, :] = num / den
        return carry

    lax.fori_loop(0, S // tile, merge_body, 0)


def _attention(proj):
    B, S, _ = proj.shape
    n_pairs = ATT_WIDTH // LANES
    n_br = len(DILATED_CONFIGS)

    def col(blk):
        return pl.BlockSpec((1, S, LANES), lambda b, p: (b, 0, blk + p))

    return pl.pallas_call(
        _att_kernel,
        out_shape=jax.ShapeDtypeStruct((B, S, ATT_WIDTH), F32),
        grid=(B, n_pairs),
        in_specs=[col(ATT_Q_BLK), col(ATT_K_BLK), col(ATT_V_BLK)],
        out_specs=pl.BlockSpec((1, S, LANES), lambda b, p: (b, 0, p)),
        scratch_shapes=[
            pltpu.VMEM((n_br, 3, 2 * N_SIDE, ATT_KEYS), F32),
            pltpu.VMEM((n_br, S, LANES), BF16),
            pltpu.VMEM((n_br, S, LANES), BF16),
            pltpu.VMEM((n_br, S, LANES), BF16),
            pltpu.VMEM((S, LANES), F32),
            pltpu.VMEM((n_br, S, LANES), F32),
            pltpu.VMEM((n_br, S, LANES), F32),
            pltpu.VMEM((n_br, S, LANES), F32),
            pltpu.VMEM((2, ATT_UNROLL, 2 * N_SIDE, ATT_KEYS), F32),
            pltpu.VMEM((2, ATT_UNROLL, 2 * N_SIDE, ATT_KEYS), BF16),
            pltpu.VMEM((2, ATT_UNROLL, 2 * N_SIDE, LANES), F32),
            pltpu.VMEM((2, ATT_UNROLL, 2 * N_SIDE, LANES), F32),
        ],
        compiler_params=pltpu.CompilerParams(
            dimension_semantics=("arbitrary", "arbitrary"), vmem_limit_bytes=VMEM_LIMIT),
        name="dilated_attention",
    )(proj, proj, proj)


def _ret_kernel(q_ref, k_ref, v_ref, g_ref, gn_ref, o_ref, sf_ref, sb_ref):
    S = q_ref.shape[1]
    C = RET_CHUNK
    n_chunks = S // C
    pair = pl.program_id(1)
    lane = lax.broadcasted_iota(jnp.int32, (1, LANES), 1)
    in_head = (lane < HEAD_DIM, lane >= HEAD_DIM)
    block = _head_block_mask((LANES, LANES))
    hsum = _head_sum_matrix()

    def log_gamma(head_f32):
        return jnp.log1p(-jnp.exp2(-5.0 - head_f32))

    zeros_row = jnp.zeros((1, LANES), F32)
    lg_head = [log_gamma((2 * pair + h).astype(F32) + zeros_row) for h in range(2)]
    lg = jnp.where(in_head[0], lg_head[0], lg_head[1])
    pos = lax.broadcasted_iota(jnp.int32, (C, LANES), 0).astype(F32)
    k_fwd = jnp.exp(lg * (C - 1.0 - pos))
    k_bwd = jnp.exp(lg * pos)
    q_fwd = jnp.exp(lg * (pos + 1.0))
    q_bwd = jnp.exp(lg * (C - pos))
    state_decay = jnp.where(block, jnp.exp(lg * C), 0.0)
    ri = lax.broadcasted_iota(jnp.int32, (C, C), 0)
    ci = lax.broadcasted_iota(jnp.int32, (C, C), 1)
    dist = jnp.abs(ri - ci).astype(F32)
    intra = [jnp.exp(lg_head[h] * dist) for h in range(2)]

    def chunk_rows(n):
        return pl.ds(pl.multiple_of(n * C, C), C)

    def incr_body(it, carry):
        for u in range(RET_UNROLL):
            n = it * RET_UNROLL + u
            rows = chunk_rows(n)
            k = k_ref[0, rows, :] * (HEAD_DIM ** -0.5)
            v = v_ref[0, rows, :]
            sf_ref[n] = jnp.where(block, _dot(k * k_fwd, v, TN_DIMS), 0.0)
            sb_ref[n] = jnp.where(block, _dot(k * k_bwd, v, TN_DIMS), 0.0)
        return carry

    lax.fori_loop(0, n_chunks // RET_UNROLL, incr_body, 0)

    def scan_fwd(n, state):
        incr = sf_ref[n]
        sf_ref[n] = state
        return state_decay * state + incr

    def scan_bwd(i, state):
        n = n_chunks - 1 - i
        incr = sb_ref[n]
        sb_ref[n] = state
        return state_decay * state + incr

    lax.fori_loop(0, n_chunks, scan_fwd, jnp.zeros((LANES, LANES), F32), unroll=4)
    lax.fori_loop(0, n_chunks, scan_bwd, jnp.zeros((LANES, LANES), F32), unroll=4)

    gn = gn_ref[...]

    def out_body(it, carry):
        rows = [chunk_rows(it * RET_UNROLL + u) for u in range(RET_UNROLL)]
        qs = [q_ref[0, r, :] for r in rows]
        vs = [v_ref[0, r, :] for r in rows]
        scores = []
        for u in range(RET_UNROLL):
            k = k_ref[0, rows[u], :] * (HEAD_DIM ** -0.5)
            scores.append([_dot(jnp.where(in_head[h], qs[u], 0.0), k, NT_DIMS) for h in range(2)])
        outs = []
        for u in range(RET_UNROLL):
            n = it * RET_UNROLL + u
            lhs = jnp.concatenate([(qs[u] * q_fwd).astype(BF16), (qs[u] * q_bwd).astype(BF16)], axis=1)
            rhs = jnp.concatenate([sf_ref[n].astype(BF16), sb_ref[n].astype(BF16)], axis=0)
            outs.append(_dot(lhs, rhs))
        for u in range(RET_UNROLL):
            lhs = jnp.concatenate([(scores[u][h] * intra[h]).astype(BF16) for h in range(2)], axis=1)
            rhs = jnp.concatenate([jnp.where(in_head[h], vs[u], 0.0).astype(BF16) for h in range(2)], axis=0)
            outs[u] = outs[u] + _dot(lhs, rhs)
        hsum2 = jnp.concatenate([hsum, hsum], axis=0)
        mus = []
        for u in range(RET_UNROLL):
            mus.append(_dot(jnp.concatenate(_split2(outs[u]), axis=1), hsum2) * (1.0 / HEAD_DIM))
        cens = [outs[u] - mus[u] for u in range(RET_UNROLL)]
        for u in range(RET_UNROLL):
            var = _dot(jnp.concatenate(_split2(cens[u] * cens[u]), axis=1), hsum2) * (1.0 / HEAD_DIM)
            g = g_ref[0, rows[u], :]
            o_ref[0, rows[u], :] = cens[u] * lax.rsqrt(var + NORM_EPS) * gn * (g * _sigmoid(g))
        return carry

    lax.fori_loop(0, n_chunks // RET_UNROLL, out_body, 0)


def _retention(proj, ret_gn):
    B, S, _ = proj.shape
    n_pairs = RET_WIDTH // LANES

    def col(blk):
        return pl.BlockSpec((1, S, LANES), lambda b, p: (b, 0, blk + p))

    return pl.pallas_call(
        _ret_kernel,
        out_shape=jax.ShapeDtypeStruct((B, S, RET_WIDTH), F32),
        grid=(B, n_pairs),
        in_specs=[col(RET_Q_BLK), col(RET_K_BLK), col(RET_V_BLK), col(RET_G_BLK),
                  pl.BlockSpec((1, LANES), lambda b, p: (0, p))],
        out_specs=pl.BlockSpec((1, S, LANES), lambda b, p: (b, 0, p)),
        scratch_shapes=[pltpu.VMEM((S // RET_CHUNK, LANES, LANES), F32),
                        pltpu.VMEM((S // RET_CHUNK, LANES, LANES), F32)],
        compiler_params=pltpu.CompilerParams(
            dimension_semantics=("arbitrary", "arbitrary"), vmem_limit_bytes=VMEM_LIMIT),
        name="retention",
    )(proj, proj, proj, proj, ret_gn.reshape(1, RET_WIDTH))


HGRN_TILE = 128
HGRN_SLABS = HGRN_TILE * (HGRN_CHUNK // SUBLANES) * 3 // 4
HGRN_SLOTS = 4


def _hgrn_kernel(layer, q_ref, zf_ref, zb_ref, i_ref, lbl_ref, o_ref,
                 qt_s, b_s, c_s, v_s, dec_s, qe_s, ke_s, w_s, mask_s):
    S = q_ref.shape[1]
    T, C = HGRN_TILE, HGRN_CHUNK
    n_tiles = S // T
    chunks = T // C
    block = _head_block_mask((LANES, LANES))
    hsum = _head_sum_matrix()

    logits = [lbl_ref[l:l + 1, :] for l in range(DEPTH)]
    mx = functools.reduce(jnp.maximum, logits)
    ex = [jnp.exp(t - mx) for t in logits]
    den = functools.reduce(lambda a, b: a + b, ex)
    probs = [e / den for e in ex]
    cum = functools.reduce(lambda a, b: a + b, probs[:layer + 1])
    lb = jnp.clip(cum - probs[0], 0.0, 1.0 - 1e-6)
    lb_floor = jnp.maximum(lb, LB_FLOOR)
    one_minus_lb = 1.0 - lb

    ri = lax.broadcasted_iota(jnp.int32, (T, T), 0)
    ci = lax.broadcasted_iota(jnp.int32, (T, T), 1)
    chunk_bits = C.bit_length() - 1
    same_chunk = jnp.right_shift(ri, chunk_bits) == jnp.right_shift(ci, chunk_bits)
    tri = (jnp.where(same_chunk & (ci <= ri), 1.0, 0.0).astype(BF16),
           jnp.where(same_chunk & (ci >= ri), 1.0, 0.0).astype(BF16))
    ones_chunk = jnp.where(same_chunk, 1.0, 0.0).astype(BF16)

    def exact_sum(mat, parts):
        return _dot(jnp.concatenate([mat] * len(parts), axis=1), jnp.concatenate(parts, axis=0))

    t_loc = lax.broadcasted_iota(jnp.int32, (SUBLANES, LANES), 0)
    for s in range(SUBLANES):
        mask_s[0, s] = jnp.where(t_loc >= s, 0.0, NEG_INF)
        mask_s[1, s] = jnp.where(t_loc <= s, 0.0, NEG_INF)

    def slab_list(backward):
        slabs = []
        for c in range(chunks):
            for s in range(C):
                for half in range(C // SUBLANES):
                    if (half <= s // SUBLANES) if backward else (half >= s // SUBLANES):
                        slabs.append((c, s, half))
        return slabs

    slabs_of = {False: slab_list(False), True: slab_list(True)}

    def tile_rows(step, backward):
        tile = (n_tiles - 1 - step) if backward else step
        if isinstance(tile, int):
            return pl.ds(tile * T, T)
        return pl.ds(pl.multiple_of(tile * T, T), T)

    def prepare(step, backward, slot):
        rows = tile_rows(step, backward)
        z = (zb_ref if backward else zf_ref)[0, rows, :]
        hq = q_ref[0, rows, :]
        q = hq * _sigmoid(hq)
        e = jnp.exp(-jnp.abs(z))
        r = 1.0 / (1.0 + e)
        er = e * r
        nonneg = z >= 0.0
        f = lb_floor + one_minus_lb * jnp.where(nonneg, r, er)
        kk = one_minus_lb * jnp.where(nonneg, er, r)
        parts = _split3(jnp.log(f))
        b2 = exact_sum(tri[1] if backward else tri[0], parts) * LOG2E
        b2_tot = exact_sum(ones_chunk, parts) * LOG2E
        qt_s[slot] = q
        b_s[slot] = b2
        c_s[slot] = b2 - jnp.log(kk) * LOG2E
        v_s[slot] = i_ref[0, rows, :]
        qe_s[slot] = (q * jnp.exp2(b2)).astype(BF16)
        ke_s[slot] = (kk * jnp.exp2(b2_tot - b2)).astype(BF16)
        dec_s[slot] = jnp.exp2(b2_tot)

    def advance(backward, slot, state):
        slabs = slabs_of[backward]
        per_chunk = len(slabs) // chunks
        attns = []
        for c in range(chunks):
            base = c * per_chunk
            for i in range(base, base + per_chunk, 2):
                ws = []
                for cc, s, half in slabs[i:i + 2]:
                    lo = cc * C + half * SUBLANES
                    c_src = jnp.broadcast_to(c_s[slot, cc * C + s:cc * C + s + 1, :], (SUBLANES, LANES))
                    expo = b_s[slot, lo:lo + SUBLANES, :] - c_src
                    if half == s // SUBLANES:
                        expo = expo + mask_s[int(backward), s % SUBLANES]
                    ws.append(qt_s[slot, lo:lo + SUBLANES, :] * jnp.exp2(expo))
                w_s[slot, i * SUBLANES:(i + 2) * SUBLANES, :] = jnp.concatenate(ws, axis=0).astype(BF16)
            attns.append(_dot(w_s[slot, base * SUBLANES:(base + per_chunk) * SUBLANES, :], hsum))

        order = range(chunks - 1, -1, -1) if backward else range(chunks)
        upds = {c: _dot(v_s[slot, c * C:(c + 1) * C, :].astype(BF16), ke_s[slot, c * C:(c + 1) * C, :], TN_DIMS)
                for c in order}
        states = {}
        for c in order:
            states[c] = state
            state = state * dec_s[slot, c * C:c * C + 1, :] + jnp.where(block, upds[c], 0.0)
        outs = {c: _dot(qe_s[slot, c * C:(c + 1) * C, :], states[c].astype(BF16), NT_DIMS) for c in order}

        pieces = []
        for c in range(chunks):
            acc = [None] * (C // SUBLANES)
            for i in range(per_chunk):
                cc, s, half = slabs[c * per_chunk + i]
                v_src = jnp.broadcast_to(v_s[slot, cc * C + s:cc * C + s + 1, :], (SUBLANES, LANES))
                term = attns[c][i * SUBLANES:(i + 1) * SUBLANES] * v_src
                acc[half] = term if acc[half] is None else acc[half] + term
            pieces.extend(acc[h] + outs[c][h * SUBLANES:(h + 1) * SUBLANES] for h in range(C // SUBLANES))
        return jnp.concatenate(pieces, axis=0), state

    for backward in (False, True):
        def trip(i, state, backward=backward):
            base = 4 * i
            outs = []
            for slot in (0, 1):
                out, state = advance(backward, slot, state)
                outs.append(out)
            prepare(base + 2, backward, 2)
            prepare(base + 3, backward, 3)
            for slot in (2, 3):
                out, state = advance(backward, slot, state)
                outs.append(out)
            prepare(jnp.minimum(base + 4, n_tiles - 1), backward, 0)
            prepare(jnp.minimum(base + 5, n_tiles - 1), backward, 1)
            for k, out in enumerate(outs):
                rows = tile_rows(base + k, backward)
                o_ref[0, rows, :] = (o_ref[0, rows, :] + out) if backward else out
            return state

        prepare(0, backward, 0)
        prepare(1, backward, 1)
        lax.fori_loop(0, n_tiles // 4, trip, jnp.zeros((LANES, LANES), F32))


def _hgrn(proj, lb_logits, layer):
    B, S, _ = proj.shape
    n_pairs = HGRN_WIDTH // LANES

    def col(blk):
        return pl.BlockSpec((1, S, LANES), lambda b, p: (b, 0, blk + p))

    return pl.pallas_call(
        functools.partial(_hgrn_kernel, layer),
        out_shape=jax.ShapeDtypeStruct((B, S, HGRN_WIDTH), F32),
        grid=(B, n_pairs),
        in_specs=[col(HG_Q_BLK), col(HG_ZF_BLK), col(HG_ZB_BLK), col(HG_I_BLK),
                  pl.BlockSpec((DEPTH, LANES), lambda b, p: (0, p))],
        out_specs=pl.BlockSpec((1, S, LANES), lambda b, p: (b, 0, p)),
        scratch_shapes=(
            [pltpu.VMEM((HGRN_SLOTS, HGRN_TILE, LANES), F32)] * 5
            + [pltpu.VMEM((HGRN_SLOTS, HGRN_TILE, LANES), BF16)] * 2
            + [pltpu.VMEM((HGRN_SLOTS, HGRN_SLABS * SUBLANES, LANES), BF16)]
            + [pltpu.VMEM((2, SUBLANES, SUBLANES, LANES), F32)]),
        compiler_params=pltpu.CompilerParams(
            dimension_semantics=("arbitrary", "arbitrary"), vmem_limit_bytes=VMEM_LIMIT),
        name="hgrn2",
    )(proj, proj, proj, proj, lb_logits)


def _ffn_kernel(final_norm, att_ref, ret_ref, hg_ref, hga_ref, hgb_ref, x_ref, wo_ref, hgn_ref,
                gate1_ref, gffn_ref, sc2_ref, sh2_ref, gate2_ref, wgu_ref, wd_ref, gfin_ref,
                o_ref, x2_s, h_s, acc_s):
    hraw = hg_ref[0]
    hgate = jnp.concatenate([hga_ref[0], hgb_ref[0]], axis=-1)
    hgo = (hraw * lax.rsqrt(jnp.mean(hraw * hraw, axis=-1, keepdims=True) + NORM_EPS)
           * hgn_ref[...] * (hgate * _sigmoid(hgate)))
    y = (_dot(att_ref[0].astype(BF16), wo_ref[0:ATT_WIDTH, :])
         + _dot(ret_ref[0].astype(BF16), wo_ref[ATT_WIDTH:ATT_WIDTH + RET_WIDTH, :])
         + _dot(hgo.astype(BF16), wo_ref[ATT_WIDTH + RET_WIDTH:, :]))
    x2 = x_ref[0] + gate1_ref[0] * y
    x2_s[...] = x2
    n = x2 * lax.rsqrt(jnp.mean(x2 * x2, axis=-1, keepdims=True) + NORM_EPS) * gffn_ref[...]
    h_s[...] = (n * (1.0 + sc2_ref[0]) + sh2_ref[0]).astype(BF16)

    lo = 0
    for width in FFN_CHUNKS:
        h = h_s[...]
        g = _dot(h, wgu_ref[:, lo:lo + width])
        u = _dot(h, wgu_ref[:, FFN_HIDDEN + lo:FFN_HIDDEN + lo + width])
        a = (g * _sigmoid(g) * u).astype(BF16)
        down = _dot(a, wd_ref[lo:lo + width, :])
        acc_s[...] = down if lo == 0 else acc_s[...] + down
        lo += width

    out = x2_s[...] + gate2_ref[0] * acc_s[...]
    if final_norm:
        out = out * lax.rsqrt(jnp.mean(out * out, axis=-1, keepdims=True) + NORM_EPS) * gfin_ref[...]
    o_ref[0] = out


def _out_proj_ffn(att, ret, hg_raw, proj, x, wo, hgrn_gn, gate1, g_ffn, scale2, shift2, gate2,
                  w_gate_up, w_down, g_final, final_norm):
    B, S, D = x.shape
    tm = 512

    def rows(width):
        return pl.BlockSpec((1, tm, width), lambda b, i: (b, i, 0))

    def per_batch():
        return pl.BlockSpec((1, 1, D), lambda b, i: (b, 0, 0))

    def vec(width):
        return pl.BlockSpec((1, width), lambda b, i: (0, 0))

    def resident(shape):
        return pl.BlockSpec(shape, lambda b, i: (0, 0), pipeline_mode=pl.Buffered(1))

    return pl.pallas_call(
        functools.partial(_ffn_kernel, final_norm),
        out_shape=jax.ShapeDtypeStruct((B, S, D), F32),
        grid=(B, S // tm),
        in_specs=[
            rows(ATT_WIDTH), rows(RET_WIDTH), rows(HGRN_WIDTH),
            pl.BlockSpec((1, tm, LANES), lambda b, i: (b, i, HG_G_BLK)),
            pl.BlockSpec((1, tm, LANES), lambda b, i: (b, i, HG_G_BLK + 1)),
            rows(D),
            resident((D, D)),
            vec(HGRN_WIDTH),
            per_batch(), vec(D), per_batch(), per_batch(), per_batch(),
            resident((D, 2 * FFN_HIDDEN)),
            resident((FFN_HIDDEN, D)),
            vec(D),
        ],
        out_specs=pl.BlockSpec((1, tm, D), lambda b, i: (b, i, 0)),
        scratch_shapes=[
            pltpu.VMEM((tm, D), F32),
            pltpu.VMEM((tm, D), BF16),
            pltpu.VMEM((tm, D), F32),
        ],
        compiler_params=pltpu.CompilerParams(
            dimension_semantics=("arbitrary", "arbitrary"), vmem_limit_bytes=VMEM_LIMIT),
        name="out_proj_ffn",
    )(att, ret, hg_raw, proj, proj, x, wo, hgrn_gn.reshape(1, HGRN_WIDTH), gate1,
      g_ffn.reshape(1, D), scale2, shift2, gate2, w_gate_up, w_down, g_final.reshape(1, D))


def kernel(x, c, w_ada, b_ada, g_mix, w_in, ret_gn, hgrn_gn, hgrn_lb_logits, w_out,
           g_ffn, w_gate_up, w_down, g_final):
    B, S, D = x.shape
    mod = _modulation(c, w_ada, b_ada).reshape(DEPTH, B, N_MOD, 1, D)
    for layer in range(DEPTH):
        shift1, scale1, gate1, shift2, scale2, gate2 = (mod[layer, :, i] for i in range(N_MOD))
        proj = _in_projection(x, g_mix[layer], scale1, shift1, w_in[layer].astype(BF16))
        att = _attention(proj)
        ret = _retention(proj, ret_gn[layer])
        hg_raw = _hgrn(proj, hgrn_lb_logits, layer)
        x = _out_proj_ffn(att, ret, hg_raw, proj, x, w_out[layer].astype(BF16), hgrn_gn[layer],
                          gate1, g_ffn[layer], scale2, shift2, gate2,
                          w_gate_up[layer].astype(BF16), w_down[layer].astype(BF16), g_final,
                          final_norm=(layer == DEPTH - 1))
    return x
```

```python
import functools

import jax
import jax.numpy as jnp
from jax import lax
from jax.experimental import pallas as pl
from jax.experimental.pallas import tpu as pltpu

D_MODEL = 1024
DEPTH = 2
HEAD_DIM = 64
ATT_WIDTH = 384
RET_WIDTH = 384
HGRN_WIDTH = 256
IN_WIDTH = 3 * ATT_WIDTH + 4 * RET_WIDTH + 5 * HGRN_WIDTH
DILATED_CONFIGS = ((128, 1), (512, 4), (2048, 16))
N_SIDE = 64
RET_CHUNK = 128
HGRN_CHUNK = 16
FFN_HIDDEN = 2816
FFN_CHUNKS = (512, 512, 512, 512, 512, 256)
N_MOD = 6
NORM_EPS = 1e-6
NEG_INF = -1e30
LB_FLOOR = 1e-30

LANES = 128
SUBLANES = 8
LOG2E = 1.4426950408889634
ATT_Q_BLK, ATT_K_BLK, ATT_V_BLK = 0, 3, 6
RET_Q_BLK, RET_K_BLK, RET_V_BLK, RET_G_BLK = 9, 12, 15, 18
HG_Q_BLK, HG_ZF_BLK, HG_ZB_BLK, HG_I_BLK, HG_G_BLK = 21, 23, 25, 27, 29

VMEM_LIMIT = 56 * 1024 * 1024

F32 = jnp.float32
BF16 = jnp.bfloat16

NT_DIMS = (((1,), (1,)), ((), ()))
TN_DIMS = (((0,), (0,)), ((), ()))


def _dot(a, b, dims=None):
    if dims is None:
        return jnp.dot(a, b, preferred_element_type=F32)
    return lax.dot_general(a, b, dims, preferred_element_type=F32)


def _split2(x):
    hi = x.astype(BF16)
    lo = (x - hi.astype(F32)).astype(BF16)
    return hi, lo


def _split3(x):
    hi = x.astype(BF16)
    r1 = x - hi.astype(F32)
    mid = r1.astype(BF16)
    lo = (r1 - mid.astype(F32)).astype(BF16)
    return hi, mid, lo


def _head_block_mask(shape):
    r = lax.broadcasted_iota(jnp.int32, shape, 0)
    c = lax.broadcasted_iota(jnp.int32, shape, 1)
    return (r >= HEAD_DIM) == (c >= HEAD_DIM)


def _head_sum_matrix():
    return jnp.where(_head_block_mask((LANES, LANES)), 1.0, 0.0).astype(BF16)


def _sigmoid(x):
    return 1.0 / (1.0 + jnp.exp(-x))


def _mod_kernel(c_ref, w_ref, b_ref, o_ref):
    c = c_ref[...]
    cond = c * _sigmoid(c)
    o_ref[0] = jnp.dot(cond, w_ref[0], preferred_element_type=F32,
                       precision=lax.Precision.HIGHEST) + b_ref[0]


def _modulation(c, w_ada, b_ada):
    B, D = c.shape
    n_out = w_ada.shape[-1]
    tn = 1536
    return pl.pallas_call(
        _mod_kernel,
        out_shape=jax.ShapeDtypeStruct((DEPTH, B, n_out), F32),
        grid=(DEPTH, n_out // tn),
        in_specs=[
            pl.BlockSpec((B, D), lambda l, j: (0, 0)),
            pl.BlockSpec((1, D, tn), lambda l, j: (l, 0, j)),
            pl.BlockSpec((1, 1, tn), lambda l, j: (l, 0, j)),
        ],
        out_specs=pl.BlockSpec((1, B, tn), lambda l, j: (l, 0, j)),
        compiler_params=pltpu.CompilerParams(
            dimension_semantics=("arbitrary", "arbitrary"), vmem_limit_bytes=VMEM_LIMIT),
        name="adaln_mod",
    )(c, w_ada, b_ada.reshape(DEPTH, 1, n_out))


def _inproj_kernel(x_ref, g_ref, sc_ref, sh_ref, w_ref, o_ref):
    x = x_ref[0]
    y = x * lax.rsqrt(jnp.mean(x * x, axis=-1, keepdims=True) + NORM_EPS) * g_ref[...]
    h = y * (1.0 + sc_ref[0]) + sh_ref[0]
    res = _dot(h.astype(BF16), w_ref[...])
    for j in range(o_ref.shape[1]):
        o_ref[0, j] = res[:, j * LANES:(j + 1) * LANES]


def _in_projection(x, gain, scale, shift, w_bf16):
    B, S, D = x.shape
    N = w_bf16.shape[-1]
    tm = 512
    return pl.pallas_call(
        _inproj_kernel,
        out_shape=jax.ShapeDtypeStruct((B, N // LANES, S, LANES), F32),
        grid=(B, S // tm),
        in_specs=[
            pl.BlockSpec((1, tm, D), lambda b, i: (b, i, 0)),
            pl.BlockSpec((1, D), lambda b, i: (0, 0)),
            pl.BlockSpec((1, 1, D), lambda b, i: (b, 0, 0)),
            pl.BlockSpec((1, 1, D), lambda b, i: (b, 0, 0)),
            pl.BlockSpec((D, N), lambda b, i: (0, 0), pipeline_mode=pl.Buffered(1)),
        ],
        out_specs=pl.BlockSpec((1, N // LANES, tm, LANES), lambda b, i: (b, 0, i, 0)),
        compiler_params=pltpu.CompilerParams(
            dimension_semantics=("arbitrary", "arbitrary"), vmem_limit_bytes=VMEM_LIMIT),
        name="in_proj",
    )(x, gain.reshape(1, D), scale, shift, w_bf16)


ATT_KEYS = 3 * N_SIDE
ATT_UNROLL = 8
RET_UNROLL = 8


def _att_kernel(q_ref, k_ref, v_ref, o_ref, bias_ref, qd_ref, kd_ref, vd_ref, t4_ref, ob_ref, mb_ref, lb_ref,
                s_ref, p_ref, m_ref, l_ref):
    S = q_ref.shape[1]
    pair = pl.program_id(1)
    lane = lax.broadcasted_iota(jnp.int32, (1, LANES), 1)
    in_head = (lane < HEAD_DIM, lane >= HEAD_DIM)
    lane_q = lax.broadcasted_iota(jnp.int32, (N_SIDE, LANES), 1)
    q_in_head = (lane_q < HEAD_DIM, lane_q >= HEAD_DIM)

    qi = lax.broadcasted_iota(jnp.int32, (N_SIDE, ATT_KEYS), 0)
    kj = lax.broadcasted_iota(jnp.int32, (N_SIDE, ATT_KEYS), 1)
    for br, (_, dil) in enumerate(DILATED_CONFIGS):
        for case in range(3):
            steps = jnp.abs(qi + N_SIDE * case - kj)
            dist = (dil * steps).astype(F32)
            for h in range(2):
                head = (2 * pair + h + 1).astype(F32) + jnp.zeros((1, ATT_KEYS), F32)
                slope = jnp.exp2(-8.0 * head / (ATT_WIDTH // HEAD_DIM)) * LOG2E
                bias_ref[br, case, h * N_SIDE:(h + 1) * N_SIDE, :] = jnp.where(
                    steps <= N_SIDE, -slope * dist, NEG_INF)

    assert [d for _, d in DILATED_CONFIGS] == [1, 4, 16]
    piece = 256
    quarter = S // 4
    for src_ref, dst_ref, scale in ((q_ref, qd_ref, HEAD_DIM ** -0.5 * LOG2E), (k_ref, kd_ref, 1.0),
                                    (v_ref, vd_ref, 1.0)):
        def pass1(j, carry, src_ref=src_ref, dst_ref=dst_ref, scale=scale):
            dst = pl.ds(pl.multiple_of(j * piece, piece), piece)
            dst_ref[0, dst, :] = (src_ref[0, dst, :] * scale).astype(BF16)
            cls = j // (quarter // piece)
            first = (j % (quarter // piece)) * piece
            x = src_ref[0, pl.ds(cls + 4 * first, piece, stride=4), :] * scale
            t4_ref[dst, :] = x
            dst_ref[1, dst, :] = x.astype(BF16)
            return carry

        lax.fori_loop(0, S // piece, pass1, 0)

        def pass2(j, carry, dst_ref=dst_ref):
            dst = pl.ds(pl.multiple_of(j * piece, piece), piece)
            x = t4_ref[pl.ds((j % 4) * quarter + j // 4, piece, stride=4), :]
            dst_ref[2, dst, :] = x.astype(BF16)
            return carry

        lax.fori_loop(0, S // piece, pass2, 0)

    zero_q = jnp.zeros((N_SIDE, LANES), BF16)
    branch_stages = []
    for br, (_, dil) in enumerate(DILATED_CONFIGS):
        sub_len = S // dil
        n_blocks = sub_len // N_SIDE
        blk_bits = n_blocks.bit_length() - 1

        n_groups = dil * n_blocks // ATT_UNROLL

        def block_index(g, u, n_blocks=n_blocks, blk_bits=blk_bits, sub_len=sub_len):
            idx = g * ATT_UNROLL + u
            c = jnp.right_shift(idx, blk_bits)
            n = jnp.bitwise_and(idx, n_blocks - 1)
            win = jnp.clip(N_SIDE * n - N_SIDE, 0, sub_len - ATT_KEYS)
            case = jnp.right_shift(N_SIDE * n - win, 6)
            q_src = pl.ds(pl.multiple_of(N_SIDE * idx, N_SIDE), N_SIDE)
            k_src = pl.ds(pl.multiple_of(N_SIDE * (idx - n) + win, N_SIDE), ATT_KEYS)
            return c, n, case, q_src, k_src

        def scores_stage(g, slot, br=br, block_index=block_index):
            for u in range(ATT_UNROLL):
                _, _, case, q_src, k_src = block_index(g, u)
                q = qd_ref[br, q_src, :]
                q2 = jnp.concatenate([jnp.where(q_in_head[h], q, zero_q) for h in range(2)], axis=0)
                s_ref[slot, u] = _dot(q2, kd_ref[br, k_src, :], NT_DIMS) + bias_ref[br, case]

        def values_stage(g, slot, br=br, dil=dil, block_index=block_index):
            for u in range(ATT_UNROLL):
                c, n, _, q_src, k_src = block_index(g, u)
                rows = q_src if dil == 1 else pl.ds(c + dil * N_SIDE * n, N_SIDE, stride=dil)
                o = _dot(p_ref[slot, u], vd_ref[br, k_src, :])
                m, l = m_ref[slot, u], l_ref[slot, u]
                ob_ref[br, rows, :] = jnp.where(in_head[0], o[:N_SIDE], o[N_SIDE:])
                mb_ref[br, rows, :] = jnp.where(in_head[0], m[:N_SIDE], m[N_SIDE:])
                lb_ref[br, rows, :] = jnp.where(in_head[0], l[:N_SIDE], l[N_SIDE:])

        branch_stages.append((scores_stage, values_stage, n_groups))

    def softmax_stage(slot):
        for u in range(ATT_UNROLL):
            s = s_ref[slot, u]
            m = jnp.max(s, axis=-1, keepdims=True)
            p = jnp.exp2(s - m)
            p_ref[slot, u] = p.astype(BF16)
            m_ref[slot, u] = jnp.broadcast_to(m, (2 * N_SIDE, LANES))
            l_ref[slot, u] = jnp.broadcast_to(jnp.sum(p, axis=-1, keepdims=True), (2 * N_SIDE, LANES))

    def tick(stage1, stage2_slot, stage3):
        if stage3 is not None:
            stage3[0](stage3[1], stage3[2])
        if stage2_slot is not None:
            softmax_stage(stage2_slot)
        if stage1 is not None:
            stage1[0](stage1[1], stage1[2])

    prev = None
    for scores_stage, values_stage, n_groups in branch_stages:
        assert n_groups % 2 == 0
        for par in range(2):
            older = None if prev is None else (prev[0], prev[1] - 2 + par, par)
            middle = (1 - par) if (prev is not None or par == 1) else None
            tick((scores_stage, par, par), middle, older)

        def trip(i, carry, scores_stage=scores_stage, values_stage=values_stage):
            for par in range(2):
                g = 2 * i + par
                tick((scores_stage, g, par), 1 - par, (values_stage, g - 2, par))
            return carry

        lax.fori_loop(1, n_groups // 2, trip, 0)
        prev = (values_stage, n_groups)
    tick(None, 1, (prev[0], prev[1] - 2, 0))
    tick(None, None, (prev[0], prev[1] - 1, 1))

    tile = 512

    def merge_body(t, carry):
        rows = pl.ds(pl.multiple_of(t * tile, tile), tile)
        ms = [mb_ref[b, rows, :] for b in range(len(DILATED_CONFIGS))]
        top = functools.reduce(jnp.maximum, ms)
        es = [jnp.exp2(m - top) for m in ms]
        num = functools.reduce(lambda a, b: a + b, [e * ob_ref[b, rows, :] for b, e in enumerate(es)])
        den = functools.reduce(lambda a, b: a + b, [e * lb_ref[b, rows, :] for b, e in enumerate(es)])
        o_ref[0, rows, :] = num / den
        return carry

    lax.fori_loop(0, S // tile, merge_body, 0)


def _attention(proj):
    B, _, S, _ = proj.shape
    n_pairs = ATT_WIDTH // LANES
    n_br = len(DILATED_CONFIGS)

    def col(blk):
        return pl.BlockSpec((None, 1, S, LANES), lambda b, p: (b, blk + p, 0, 0))

    return pl.pallas_call(
        _att_kernel,
        out_shape=jax.ShapeDtypeStruct((B, ATT_WIDTH // LANES, S, LANES), F32),
        grid=(B, n_pairs),
        in_specs=[col(ATT_Q_BLK), col(ATT_K_BLK), col(ATT_V_BLK)],
        out_specs=pl.BlockSpec((None, 1, S, LANES), lambda b, p: (b, p, 0, 0)),
        scratch_shapes=[
            pltpu.VMEM((n_br, 3, 2 * N_SIDE, ATT_KEYS), F32),
            pltpu.VMEM((n_br, S, LANES), BF16),
            pltpu.VMEM((n_br, S, LANES), BF16),
            pltpu.VMEM((n_br, S, LANES), BF16),
            pltpu.VMEM((S, LANES), F32),
            pltpu.VMEM((n_br, S, LANES), F32),
            pltpu.VMEM((n_br, S, LANES), F32),
            pltpu.VMEM((n_br, S, LANES), F32),
            pltpu.VMEM((2, ATT_UNROLL, 2 * N_SIDE, ATT_KEYS), F32),
            pltpu.VMEM((2, ATT_UNROLL, 2 * N_SIDE, ATT_KEYS), BF16),
            pltpu.VMEM((2, ATT_UNROLL, 2 * N_SIDE, LANES), F32),
            pltpu.VMEM((2, ATT_UNROLL, 2 * N_SIDE, LANES), F32),
        ],
        compiler_params=pltpu.CompilerParams(
            dimension_semantics=("arbitrary", "arbitrary"), vmem_limit_bytes=VMEM_LIMIT),
        name="dilated_attention",
    )(proj, proj, proj)


def _ret_kernel(q_ref, k_ref, v_ref, g_ref, gn_ref, o_ref, sf_ref, sb_ref):
    S = q_ref.shape[1]
    C = RET_CHUNK
    n_chunks = S // C
    pair = pl.program_id(1)
    lane = lax.broadcasted_iota(jnp.int32, (1, LANES), 1)
    in_head = (lane < HEAD_DIM, lane >= HEAD_DIM)
    block = _head_block_mask((LANES, LANES))
    hsum = _head_sum_matrix()

    def log_gamma(head_f32):
        return jnp.log1p(-jnp.exp2(-5.0 - head_f32))

    zeros_row = jnp.zeros((1, LANES), F32)
    lg_head = [log_gamma((2 * pair + h).astype(F32) + zeros_row) for h in range(2)]
    lg = jnp.where(in_head[0], lg_head[0], lg_head[1])
    pos = lax.broadcasted_iota(jnp.int32, (C, LANES), 0).astype(F32)
    k_fwd = jnp.exp(lg * (C - 1.0 - pos))
    k_bwd = jnp.exp(lg * pos)
    q_fwd = jnp.exp(lg * (pos + 1.0))
    q_bwd = jnp.exp(lg * (C - pos))
    state_decay = jnp.where(block, jnp.exp(lg * C), 0.0)
    ri = lax.broadcasted_iota(jnp.int32, (C, C), 0)
    ci = lax.broadcasted_iota(jnp.int32, (C, C), 1)
    dist = jnp.abs(ri - ci).astype(F32)
    intra = [jnp.exp(lg_head[h] * dist) for h in range(2)]

    def chunk_rows(n):
        return pl.ds(pl.multiple_of(n * C, C), C)

    def incr_body(it, carry):
        for u in range(RET_UNROLL):
            n = it * RET_UNROLL + u
            rows = chunk_rows(n)
            k = k_ref[0, rows, :] * (HEAD_DIM ** -0.5)
            v = v_ref[0, rows, :]
            sf_ref[n] = jnp.where(block, _dot(k * k_fwd, v, TN_DIMS), 0.0)
            sb_ref[n] = jnp.where(block, _dot(k * k_bwd, v, TN_DIMS), 0.0)
        return carry

    lax.fori_loop(0, n_chunks // RET_UNROLL, incr_body, 0)

    def scan_fwd(n, state):
        incr = sf_ref[n]
        sf_ref[n] = state
        return state_decay * state + incr

    def scan_bwd(i, state):
        n = n_chunks - 1 - i
        incr = sb_ref[n]
        sb_ref[n] = state
        return state_decay * state + incr

    lax.fori_loop(0, n_chunks, scan_fwd, jnp.zeros((LANES, LANES), F32), unroll=4)
    lax.fori_loop(0, n_chunks, scan_bwd, jnp.zeros((LANES, LANES), F32), unroll=4)

    gn = gn_ref[...]

    def out_body(it, carry):
        rows = [chunk_rows(it * RET_UNROLL + u) for u in range(RET_UNROLL)]
        qs = [q_ref[0, r, :] for r in rows]
        vs = [v_ref[0, r, :] for r in rows]
        scores = []
        for u in range(RET_UNROLL):
            k = k_ref[0, rows[u], :] * (HEAD_DIM ** -0.5)
            scores.append([_dot(jnp.where(in_head[h], qs[u], 0.0), k, NT_DIMS) for h in range(2)])
        outs = []
        for u in range(RET_UNROLL):
            n = it * RET_UNROLL + u
            lhs = jnp.concatenate([(qs[u] * q_fwd).astype(BF16), (qs[u] * q_bwd).astype(BF16)], axis=1)
            rhs = jnp.concatenate([sf_ref[n].astype(BF16), sb_ref[n].astype(BF16)], axis=0)
            outs.append(_dot(lhs, rhs))
        for u in range(RET_UNROLL):
            lhs = jnp.concatenate([(scores[u][h] * intra[h]).astype(BF16) for h in range(2)], axis=1)
            rhs = jnp.concatenate([jnp.where(in_head[h], vs[u], 0.0).astype(BF16) for h in range(2)], axis=0)
            outs[u] = outs[u] + _dot(lhs, rhs)
        hsum2 = jnp.concatenate([hsum, hsum], axis=0)
        mus = []
        for u in range(RET_UNROLL):
            mus.append(_dot(jnp.concatenate(_split2(outs[u]), axis=1), hsum2) * (1.0 / HEAD_DIM))
        cens = [outs[u] - mus[u] for u in range(RET_UNROLL)]
        for u in range(RET_UNROLL):
            var = _dot(jnp.concatenate(_split2(cens[u] * cens[u]), axis=1), hsum2) * (1.0 / HEAD_DIM)
            g = g_ref[0, rows[u], :]
            o_ref[0, rows[u], :] = cens[u] * lax.rsqrt(var + NORM_EPS) * gn * (g * _sigmoid(g))
        return carry

    lax.fori_loop(0, n_chunks // RET_UNROLL, out_body, 0)


def _retention(proj, ret_gn):
    B, _, S, _ = proj.shape
    n_pairs = RET_WIDTH // LANES

    def col(blk):
        return pl.BlockSpec((None, 1, S, LANES), lambda b, p: (b, blk + p, 0, 0))

    return pl.pallas_call(
        _ret_kernel,
        out_shape=jax.ShapeDtypeStruct((B, RET_WIDTH // LANES, S, LANES), F32),
        grid=(B, n_pairs),
        in_specs=[col(RET_Q_BLK), col(RET_K_BLK), col(RET_V_BLK), col(RET_G_BLK),
                  pl.BlockSpec((1, LANES), lambda b, p: (0, p))],
        out_specs=pl.BlockSpec((None, 1, S, LANES), lambda b, p: (b, p, 0, 0)),
        scratch_shapes=[pltpu.VMEM((S // RET_CHUNK, LANES, LANES), F32),
                        pltpu.VMEM((S // RET_CHUNK, LANES, LANES), F32)],
        compiler_params=pltpu.CompilerParams(
            dimension_semantics=("arbitrary", "arbitrary"), vmem_limit_bytes=VMEM_LIMIT),
        name="retention",
    )(proj, proj, proj, proj, ret_gn.reshape(1, RET_WIDTH))


HGRN_TILE = 128
HGRN_SLABS = HGRN_TILE * (HGRN_CHUNK // SUBLANES) * 3 // 4
HGRN_SLOTS = 4


def _hgrn_kernel(layer, q_ref, zf_ref, zb_ref, i_ref, lbl_ref, o_ref,
                 qt_s, b_s, c_s, v_s, dec_s, qe_s, ke_s, w_s, mask_s):
    S = q_ref.shape[1]
    T, C = HGRN_TILE, HGRN_CHUNK
    n_tiles = S // T
    chunks = T // C
    block = _head_block_mask((LANES, LANES))
    hsum = _head_sum_matrix()

    logits = [lbl_ref[l:l + 1, :] for l in range(DEPTH)]
    mx = functools.reduce(jnp.maximum, logits)
    ex = [jnp.exp(t - mx) for t in logits]
    den = functools.reduce(lambda a, b: a + b, ex)
    probs = [e / den for e in ex]
    cum = functools.reduce(lambda a, b: a + b, probs[:layer + 1])
    lb = jnp.clip(cum - probs[0], 0.0, 1.0 - 1e-6)
    lb_floor = jnp.maximum(lb, LB_FLOOR)
    one_minus_lb = 1.0 - lb

    ri = lax.broadcasted_iota(jnp.int32, (T, T), 0)
    ci = lax.broadcasted_iota(jnp.int32, (T, T), 1)
    chunk_bits = C.bit_length() - 1
    same_chunk = jnp.right_shift(ri, chunk_bits) == jnp.right_shift(ci, chunk_bits)
    tri = (jnp.where(same_chunk & (ci <= ri), 1.0, 0.0).astype(BF16),
           jnp.where(same_chunk & (ci >= ri), 1.0, 0.0).astype(BF16))
    ones_chunk = jnp.where(same_chunk, 1.0, 0.0).astype(BF16)

    def exact_sum(mat, parts):
        return _dot(jnp.concatenate([mat] * len(parts), axis=1), jnp.concatenate(parts, axis=0))

    t_loc = lax.broadcasted_iota(jnp.int32, (SUBLANES, LANES), 0)
    for s in range(SUBLANES):
        mask_s[0, s] = jnp.where(t_loc >= s, 0.0, NEG_INF)
        mask_s[1, s] = jnp.where(t_loc <= s, 0.0, NEG_INF)

    def slab_list(backward):
        slabs = []
        for c in range(chunks):
            for s in range(C):
                for half in range(C // SUBLANES):
                    if (half <= s // SUBLANES) if backward else (half >= s // SUBLANES):
                        slabs.append((c, s, half))
        return slabs

    slabs_of = {False: slab_list(False), True: slab_list(True)}

    def tile_rows(step, backward):
        tile = (n_tiles - 1 - step) if backward else step
        if isinstance(tile, int):
            return pl.ds(tile * T, T)
        return pl.ds(pl.multiple_of(tile * T, T), T)

    def prepare(step, backward, slot):
        rows = tile_rows(step, backward)
        z = (zb_ref if backward else zf_ref)[0, rows, :]
        hq = q_ref[0, rows, :]
        q = hq * _sigmoid(hq)
        e = jnp.exp(-jnp.abs(z))
        r = 1.0 / (1.0 + e)
        er = e * r
        nonneg = z >= 0.0
        f = lb_floor + one_minus_lb * jnp.where(nonneg, r, er)
        kk = one_minus_lb * jnp.where(nonneg, er, r)
        parts = _split3(jnp.log(f))
        b2 = exact_sum(tri[1] if backward else tri[0], parts) * LOG2E
        b2_tot = exact_sum(ones_chunk, parts) * LOG2E
        qt_s[slot] = q
        b_s[slot] = b2
        c_s[slot] = b2 - jnp.log(kk) * LOG2E
        v_s[slot] = i_ref[0, rows, :]
        qe_s[slot] = (q * jnp.exp2(b2)).astype(BF16)
        ke_s[slot] = (kk * jnp.exp2(b2_tot - b2)).astype(BF16)
        dec_s[slot] = jnp.exp2(b2_tot)

    def advance(backward, slot, state):
        slabs = slabs_of[backward]
        per_chunk = len(slabs) // chunks
        attns = []
        for c in range(chunks):
            base = c * per_chunk
            for i in range(base, base + per_chunk, 2):
                ws = []
                for cc, s, half in slabs[i:i + 2]:
                    lo = cc * C + half * SUBLANES
                    c_src = jnp.broadcast_to(c_s[slot, cc * C + s:cc * C + s + 1, :], (SUBLANES, LANES))
                    expo = b_s[slot, lo:lo + SUBLANES, :] - c_src
                    if half == s // SUBLANES:
                        expo = expo + mask_s[int(backward), s % SUBLANES]
                    ws.append(qt_s[slot, lo:lo + SUBLANES, :] * jnp.exp2(expo))
                w_s[slot, i * SUBLANES:(i + 2) * SUBLANES, :] = jnp.concatenate(ws, axis=0).astype(BF16)
            attns.append(_dot(w_s[slot, base * SUBLANES:(base + per_chunk) * SUBLANES, :], hsum))

        order = range(chunks - 1, -1, -1) if backward else range(chunks)
        upds = {c: _dot(v_s[slot, c * C:(c + 1) * C, :].astype(BF16), ke_s[slot, c * C:(c + 1) * C, :], TN_DIMS)
                for c in order}
        states = {}
        for c in order:
            states[c] = state
            state = state * dec_s[slot, c * C:c * C + 1, :] + jnp.where(block, upds[c], 0.0)
        outs = {c: _dot(qe_s[slot, c * C:(c + 1) * C, :], states[c].astype(BF16), NT_DIMS) for c in order}

        pieces = []
        for c in range(chunks):
            acc = [None] * (C // SUBLANES)
            for i in range(per_chunk):
                cc, s, half = slabs[c * per_chunk + i]
                v_src = jnp.broadcast_to(v_s[slot, cc * C + s:cc * C + s + 1, :], (SUBLANES, LANES))
                term = attns[c][i * SUBLANES:(i + 1) * SUBLANES] * v_src
                acc[half] = term if acc[half] is None else acc[half] + term
            pieces.extend(acc[h] + outs[c][h * SUBLANES:(h + 1) * SUBLANES] for h in range(C // SUBLANES))
        return jnp.concatenate(pieces, axis=0), state

    for backward in (False, True):
        def trip(i, state, backward=backward):
            base = 4 * i
            outs = []
            for slot in (0, 1):
                out, state = advance(backward, slot, state)
                outs.append(out)
            prepare(base + 2, backward, 2)
            prepare(base + 3, backward, 3)
            for slot in (2, 3):
                out, state = advance(backward, slot, state)
                outs.append(out)
            prepare(jnp.minimum(base + 4, n_tiles - 1), backward, 0)
            prepare(jnp.minimum(base + 5, n_tiles - 1), backward, 1)
            for k, out in enumerate(outs):
                rows = tile_rows(base + k, backward)
                o_ref[0, rows, :] = (o_ref[0, rows, :] + out) if backward else out
            return state

        prepare(0, backward, 0)
        prepare(1, backward, 1)
        lax.fori_loop(0, n_tiles // 4, trip, jnp.zeros((LANES, LANES), F32))


def _hgrn(proj, lb_logits, layer):
    B, _, S, _ = proj.shape
    n_pairs = HGRN_WIDTH // LANES

    def col(blk):
        return pl.BlockSpec((None, 1, S, LANES), lambda b, p: (b, blk + p, 0, 0))

    return pl.pallas_call(
        functools.partial(_hgrn_kernel, layer),
        out_shape=jax.ShapeDtypeStruct((B, HGRN_WIDTH // LANES, S, LANES), F32),
        grid=(B, n_pairs),
        in_specs=[col(HG_Q_BLK), col(HG_ZF_BLK), col(HG_ZB_BLK), col(HG_I_BLK),
                  pl.BlockSpec((DEPTH, LANES), lambda b, p: (0, p))],
        out_specs=pl.BlockSpec((None, 1, S, LANES), lambda b, p: (b, p, 0, 0)),
        scratch_shapes=(
            [pltpu.VMEM((HGRN_SLOTS, HGRN_TILE, LANES), F32)] * 5
            + [pltpu.VMEM((HGRN_SLOTS, HGRN_TILE, LANES), BF16)] * 2
            + [pltpu.VMEM((HGRN_SLOTS, HGRN_SLABS * SUBLANES, LANES), BF16)]
            + [pltpu.VMEM((2, SUBLANES, SUBLANES, LANES), F32)]),
        compiler_params=pltpu.CompilerParams(
            dimension_semantics=("arbitrary", "arbitrary"), vmem_limit_bytes=VMEM_LIMIT),
        name="hgrn2",
    )(proj, proj, proj, proj, lb_logits)


def _ffn_kernel(final_norm, att_ref, ret_ref, hg_ref, hga_ref, hgb_ref, x_ref, wo_ref, hgn_ref,
                gate1_ref, gffn_ref, sc2_ref, sh2_ref, gate2_ref, wgu_ref, wd_ref, gfin_ref,
                o_ref, x2_s, h_s, acc_s):
    def wide(ref):
        return jnp.concatenate([ref[0, j] for j in range(ref.shape[1])], axis=-1)

    hraw = wide(hg_ref)
    hgate = jnp.concatenate([hga_ref[0], hgb_ref[0]], axis=-1)
    hgo = (hraw * lax.rsqrt(jnp.mean(hraw * hraw, axis=-1, keepdims=True) + NORM_EPS)
           * hgn_ref[...] * (hgate * _sigmoid(hgate)))
    mixed = jnp.concatenate([wide(att_ref).astype(BF16), wide(ret_ref).astype(BF16), hgo.astype(BF16)], axis=-1)
    y = _dot(mixed, wo_ref[...])
    x2 = x_ref[0] + gate1_ref[0] * y
    x2_s[...] = x2
    n = x2 * lax.rsqrt(jnp.mean(x2 * x2, axis=-1, keepdims=True) + NORM_EPS) * gffn_ref[...]
    h_s[...] = (n * (1.0 + sc2_ref[0]) + sh2_ref[0]).astype(BF16)

    lo = 0
    for width in FFN_CHUNKS:
        h = h_s[...]
        g = _dot(h, wgu_ref[:, lo:lo + width])
        u = _dot(h, wgu_ref[:, FFN_HIDDEN + lo:FFN_HIDDEN + lo + width])
        a = (g * _sigmoid(g) * u).astype(BF16)
        down = _dot(a, wd_ref[lo:lo + width, :])
        acc_s[...] = down if lo == 0 else acc_s[...] + down
        lo += width

    out = x2_s[...] + gate2_ref[0] * acc_s[...]
    if final_norm:
        out = out * lax.rsqrt(jnp.mean(out * out, axis=-1, keepdims=True) + NORM_EPS) * gfin_ref[...]
    o_ref[0] = out


def _out_proj_ffn(att, ret, hg_raw, proj, x, wo, hgrn_gn, gate1, g_ffn, scale2, shift2, gate2,
                  w_gate_up, w_down, g_final, final_norm):
    B, S, D = x.shape
    tm = 512

    def rows(width):
        return pl.BlockSpec((1, tm, width), lambda b, i: (b, i, 0))

    def blocks(width):
        return pl.BlockSpec((1, width // LANES, tm, LANES), lambda b, i: (b, 0, i, 0))

    def per_batch():
        return pl.BlockSpec((1, 1, D), lambda b, i: (b, 0, 0))

    def vec(width):
        return pl.BlockSpec((1, width), lambda b, i: (0, 0))

    def resident(shape):
        return pl.BlockSpec(shape, lambda b, i: (0, 0), pipeline_mode=pl.Buffered(1))

    return pl.pallas_call(
        functools.partial(_ffn_kernel, final_norm),
        out_shape=jax.ShapeDtypeStruct((B, S, D), F32),
        grid=(B, S // tm),
        in_specs=[
            blocks(ATT_WIDTH), blocks(RET_WIDTH), blocks(HGRN_WIDTH),
            pl.BlockSpec((None, 1, tm, LANES), lambda b, i: (b, HG_G_BLK, i, 0)),
            pl.BlockSpec((None, 1, tm, LANES), lambda b, i: (b, HG_G_BLK + 1, i, 0)),
            rows(D),
            resident((D, D)),
            vec(HGRN_WIDTH),
            per_batch(), vec(D), per_batch(), per_batch(), per_batch(),
            resident((D, 2 * FFN_HIDDEN)),
            resident((FFN_HIDDEN, D)),
            vec(D),
        ],
        out_specs=pl.BlockSpec((1, tm, D), lambda b, i: (b, i, 0)),
        scratch_shapes=[
            pltpu.VMEM((tm, D), F32),
            pltpu.VMEM((tm, D), BF16),
            pltpu.VMEM((tm, D), F32),
        ],
        compiler_params=pltpu.CompilerParams(
            dimension_semantics=("arbitrary", "arbitrary"), vmem_limit_bytes=VMEM_LIMIT),
        name="out_proj_ffn",
    )(att, ret, hg_raw, proj, proj, x, wo, hgrn_gn.reshape(1, HGRN_WIDTH), gate1,
      g_ffn.reshape(1, D), scale2, shift2, gate2, w_gate_up, w_down, g_final.reshape(1, D))


def kernel(x, c, w_ada, b_ada, g_mix, w_in, ret_gn, hgrn_gn, hgrn_lb_logits, w_out,
           g_ffn, w_gate_up, w_down, g_final):
    B, S, D = x.shape
    mod = _modulation(c, w_ada, b_ada).reshape(DEPTH, B, N_MOD, 1, D)
    for layer in range(DEPTH):
        shift1, scale1, gate1, shift2, scale2, gate2 = (mod[layer, :, i] for i in range(N_MOD))
        proj = _in_projection(x, g_mix[layer], scale1, shift1, w_in[layer].astype(BF16))
        att = _attention(proj)
        ret = _retention(proj, ret_gn[layer])
        hg_raw = _hgrn(proj, hgrn_lb_logits, layer)
        x = _out_proj_ffn(att, ret, hg_raw, proj, x, w_out[layer].astype(BF16), hgrn_gn[layer],
                          gate1, g_ffn[layer], scale2, shift2, gate2,
                          w_gate_up[layer].astype(BF16), w_down[layer].astype(BF16), g_final,
                          final_norm=(layer == DEPTH - 1))
    return x
```

```python
import functools

import jax
import jax.numpy as jnp
from jax import lax
from jax.experimental import pallas as pl
from jax.experimental.pallas import tpu as pltpu

D_MODEL = 1024
DEPTH = 2
HEAD_DIM = 64
ATT_WIDTH = 384
RET_WIDTH = 384
HGRN_WIDTH = 256
IN_WIDTH = 3 * ATT_WIDTH + 4 * RET_WIDTH + 5 * HGRN_WIDTH
DILATED_CONFIGS = ((128, 1), (512, 4), (2048, 16))
N_SIDE = 64
RET_CHUNK = 128
HGRN_CHUNK = 16
FFN_HIDDEN = 2816
FFN_CHUNKS = (512, 512, 512, 512, 512, 256)
N_MOD = 6
NORM_EPS = 1e-6
NEG_INF = -1e30
LB_FLOOR = 1e-30

LANES = 128
SUBLANES = 8
LOG2E = 1.4426950408889634
ATT_Q_BLK, ATT_K_BLK, ATT_V_BLK = 0, 3, 6
RET_Q_BLK, RET_K_BLK, RET_V_BLK, RET_G_BLK = 9, 12, 15, 18
HG_Q_BLK, HG_ZF_BLK, HG_ZB_BLK, HG_I_BLK, HG_G_BLK = 21, 23, 25, 27, 29

VMEM_LIMIT = 56 * 1024 * 1024

F32 = jnp.float32
BF16 = jnp.bfloat16

NT_DIMS = (((1,), (1,)), ((), ()))
TN_DIMS = (((0,), (0,)), ((), ()))


def _dot(a, b, dims=None):
    if dims is None:
        return jnp.dot(a, b, preferred_element_type=F32)
    return lax.dot_general(a, b, dims, preferred_element_type=F32)


def _split2(x):
    hi = x.astype(BF16)
    lo = (x - hi.astype(F32)).astype(BF16)
    return hi, lo


def _split3(x):
    hi = x.astype(BF16)
    r1 = x - hi.astype(F32)
    mid = r1.astype(BF16)
    lo = (r1 - mid.astype(F32)).astype(BF16)
    return hi, mid, lo


def _head_block_mask(shape):
    r = lax.broadcasted_iota(jnp.int32, shape, 0)
    c = lax.broadcasted_iota(jnp.int32, shape, 1)
    return (r >= HEAD_DIM) == (c >= HEAD_DIM)


def _head_sum_matrix():
    return jnp.where(_head_block_mask((LANES, LANES)), 1.0, 0.0).astype(BF16)


def _sigmoid(x):
    return 1.0 / (1.0 + jnp.exp(-x))


def _mod_kernel(c_ref, w_ref, b_ref, o_ref):
    c = c_ref[...]
    cond = c * _sigmoid(c)
    o_ref[0] = jnp.dot(cond, w_ref[0], preferred_element_type=F32,
                       precision=lax.Precision.HIGHEST) + b_ref[0]


def _modulation(c, w_ada, b_ada):
    B, D = c.shape
    n_out = w_ada.shape[-1]
    tn = 1536
    return pl.pallas_call(
        _mod_kernel,
        out_shape=jax.ShapeDtypeStruct((DEPTH, B, n_out), F32),
        grid=(DEPTH, n_out // tn),
        in_specs=[
            pl.BlockSpec((B, D), lambda l, j: (0, 0)),
            pl.BlockSpec((1, D, tn), lambda l, j: (l, 0, j)),
            pl.BlockSpec((1, 1, tn), lambda l, j: (l, 0, j)),
        ],
        out_specs=pl.BlockSpec((1, B, tn), lambda l, j: (l, 0, j)),
        compiler_params=pltpu.CompilerParams(
            dimension_semantics=("arbitrary", "arbitrary"), vmem_limit_bytes=VMEM_LIMIT),
        name="adaln_mod",
    )(c, w_ada, b_ada.reshape(DEPTH, 1, n_out))


def _inproj_kernel(x_ref, g_ref, sc_ref, sh_ref, w_ref, o_ref):
    x = x_ref[0]
    y = x * lax.rsqrt(jnp.mean(x * x, axis=-1, keepdims=True) + NORM_EPS) * g_ref[...]
    h = y * (1.0 + sc_ref[0]) + sh_ref[0]
    res = _dot(h.astype(BF16), w_ref[...])
    for j in range(o_ref.shape[1]):
        o_ref[0, j] = res[:, j * LANES:(j + 1) * LANES]


def _in_projection(x, gain, scale, shift, w_bf16):
    B, S, D = x.shape
    N = w_bf16.shape[-1]
    tm = 512
    return pl.pallas_call(
        _inproj_kernel,
        out_shape=jax.ShapeDtypeStruct((B, N // LANES, S, LANES), F32),
        grid=(B, S // tm),
        in_specs=[
            pl.BlockSpec((1, tm, D), lambda b, i: (b, i, 0)),
            pl.BlockSpec((1, D), lambda b, i: (0, 0)),
            pl.BlockSpec((1, 1, D), lambda b, i: (b, 0, 0)),
            pl.BlockSpec((1, 1, D), lambda b, i: (b, 0, 0)),
            pl.BlockSpec((D, N), lambda b, i: (0, 0), pipeline_mode=pl.Buffered(1)),
        ],
        out_specs=pl.BlockSpec((1, N // LANES, tm, LANES), lambda b, i: (b, 0, i, 0)),
        compiler_params=pltpu.CompilerParams(
            dimension_semantics=("arbitrary", "arbitrary"), vmem_limit_bytes=VMEM_LIMIT),
        name="in_proj",
    )(x, gain.reshape(1, D), scale, shift, w_bf16)


ATT_KEYS = 3 * N_SIDE
ATT_UNROLL = 8
RET_UNROLL = 8


def _att_kernel(q_ref, k_ref, v_ref, o_ref, bias_ref, qd_ref, kd_ref, vd_ref, t4_ref, ob_ref, mb_ref, lb_ref,
                s_ref, p_ref, m_ref):
    S = q_ref.shape[1]
    pair = pl.program_id(1)
    lane = lax.broadcasted_iota(jnp.int32, (1, LANES), 1)
    in_head = (lane < HEAD_DIM, lane >= HEAD_DIM)
    lane_q = lax.broadcasted_iota(jnp.int32, (N_SIDE, LANES), 1)
    q_in_head = (lane_q < HEAD_DIM, lane_q >= HEAD_DIM)

    qi = lax.broadcasted_iota(jnp.int32, (N_SIDE, ATT_KEYS), 0)
    kj = lax.broadcasted_iota(jnp.int32, (N_SIDE, ATT_KEYS), 1)
    for br, (_, dil) in enumerate(DILATED_CONFIGS):
        for case in range(3):
            steps = jnp.abs(qi + N_SIDE * case - kj)
            dist = (dil * steps).astype(F32)
            for h in range(2):
                head = (2 * pair + h + 1).astype(F32) + jnp.zeros((1, ATT_KEYS), F32)
                slope = jnp.exp2(-8.0 * head / (ATT_WIDTH // HEAD_DIM)) * LOG2E
                bias_ref[br, case, h * N_SIDE:(h + 1) * N_SIDE, :] = jnp.where(
                    steps <= N_SIDE, -slope * dist, NEG_INF)

    assert [d for _, d in DILATED_CONFIGS] == [1, 4, 16]
    piece = 256
    quarter = S // 4
    for src_ref, dst_ref, scale in ((q_ref, qd_ref, HEAD_DIM ** -0.5 * LOG2E), (k_ref, kd_ref, 1.0),
                                    (v_ref, vd_ref, 1.0)):
        def pass1(j, carry, src_ref=src_ref, dst_ref=dst_ref, scale=scale):
            dst = pl.ds(pl.multiple_of(j * piece, piece), piece)
            dst_ref[0, dst, :] = (src_ref[0, dst, :] * scale).astype(BF16)
            cls = j // (quarter // piece)
            first = (j % (quarter // piece)) * piece
            x = src_ref[0, pl.ds(cls + 4 * first, piece, stride=4), :] * scale
            t4_ref[dst, :] = x
            dst_ref[1, dst, :] = x.astype(BF16)
            return carry

        lax.fori_loop(0, S // piece, pass1, 0)

        def pass2(j, carry, dst_ref=dst_ref):
            dst = pl.ds(pl.multiple_of(j * piece, piece), piece)
            x = t4_ref[pl.ds((j % 4) * quarter + j // 4, piece, stride=4), :]
            dst_ref[2, dst, :] = x.astype(BF16)
            return carry

        lax.fori_loop(0, S // piece, pass2, 0)

    zero_q = jnp.zeros((N_SIDE, LANES), BF16)
    ones_v = jnp.ones((ATT_KEYS, LANES), BF16)
    branch_stages = []
    for br, (_, dil) in enumerate(DILATED_CONFIGS):
        sub_len = S // dil
        n_blocks = sub_len // N_SIDE
        blk_bits = n_blocks.bit_length() - 1

        n_groups = dil * n_blocks // ATT_UNROLL

        def block_index(g, u, n_blocks=n_blocks, blk_bits=blk_bits, sub_len=sub_len):
            idx = g * ATT_UNROLL + u
            c = jnp.right_shift(idx, blk_bits)
            n = jnp.bitwise_and(idx, n_blocks - 1)
            win = jnp.clip(N_SIDE * n - N_SIDE, 0, sub_len - ATT_KEYS)
            case = jnp.right_shift(N_SIDE * n - win, 6)
            q_src = pl.ds(pl.multiple_of(N_SIDE * idx, N_SIDE), N_SIDE)
            k_src = pl.ds(pl.multiple_of(N_SIDE * (idx - n) + win, N_SIDE), ATT_KEYS)
            return c, n, case, q_src, k_src

        def scores_stage(g, slot, br=br, block_index=block_index):
            for u in range(ATT_UNROLL):
                _, _, case, q_src, k_src = block_index(g, u)
                q = qd_ref[br, q_src, :]
                q2 = jnp.concatenate([jnp.where(q_in_head[h], q, zero_q) for h in range(2)], axis=0)
                s_ref[slot, u] = _dot(q2, kd_ref[br, k_src, :], NT_DIMS) + bias_ref[br, case]

        def values_stage(g, slot, br=br, dil=dil, block_index=block_index):
            for u in range(ATT_UNROLL):
                c, n, _, q_src, k_src = block_index(g, u)
                rows = q_src if dil == 1 else pl.ds(c + dil * N_SIDE * n, N_SIDE, stride=dil)
                ol = _dot(p_ref[slot, u], jnp.concatenate([vd_ref[br, k_src, :], ones_v], axis=1))
                o, l = ol[:, :LANES], ol[:, LANES:]
                m = m_ref[slot, u]
                ob_ref[br, rows, :] = jnp.where(in_head[0], o[:N_SIDE], o[N_SIDE:])
                mb_ref[br, rows, :] = jnp.where(in_head[0], m[:N_SIDE], m[N_SIDE:])
                lb_ref[br, rows, :] = jnp.where(in_head[0], l[:N_SIDE], l[N_SIDE:])

        branch_stages.append((scores_stage, values_stage, n_groups))

    def softmax_stage(slot):
        for u in range(ATT_UNROLL):
            s = s_ref[slot, u]
            m = jnp.max(s, axis=-1, keepdims=True)
            p = jnp.exp2(s - m)
            p_ref[slot, u] = p.astype(BF16)
            m_ref[slot, u] = jnp.broadcast_to(m, (2 * N_SIDE, LANES))

    def tick(stage1, stage2_slot, stage3):
        if stage3 is not None:
            stage3[0](stage3[1], stage3[2])
        if stage2_slot is not None:
            softmax_stage(stage2_slot)
        if stage1 is not None:
            stage1[0](stage1[1], stage1[2])

    prev = None
    for scores_stage, values_stage, n_groups in branch_stages:
        assert n_groups % 2 == 0
        for par in range(2):
            older = None if prev is None else (prev[0], prev[1] - 2 + par, par)
            middle = (1 - par) if (prev is not None or par == 1) else None
            tick((scores_stage, par, par), middle, older)

        def trip(i, carry, scores_stage=scores_stage, values_stage=values_stage):
            for par in range(2):
                g = 2 * i + par
                tick((scores_stage, g, par), 1 - par, (values_stage, g - 2, par))
            return carry

        lax.fori_loop(1, n_groups // 2, trip, 0)
        prev = (values_stage, n_groups)
    tick(None, 1, (prev[0], prev[1] - 2, 0))
    tick(None, None, (prev[0], prev[1] - 1, 1))

    tile = 512

    def merge_body(t, carry):
        rows = pl.ds(pl.multiple_of(t * tile, tile), tile)
        ms = [mb_ref[b, rows, :] for b in range(len(DILATED_CONFIGS))]
        top = functools.reduce(jnp.maximum, ms)
        es = [jnp.exp2(m - top) for m in ms]
        num = functools.reduce(lambda a, b: a + b, [e * ob_ref[b, rows, :] for b, e in enumerate(es)])
        den = functools.reduce(lambda a, b: a + b, [e * lb_ref[b, rows, :] for b, e in enumerate(es)])
        o_ref[0, rows, :] = num / den
        return carry

    lax.fori_loop(0, S // tile, merge_body, 0)


def _attention(proj):
    B, _, S, _ = proj.shape
    n_pairs = ATT_WIDTH // LANES
    n_br = len(DILATED_CONFIGS)

    def col(blk):
        return pl.BlockSpec((None, 1, S, LANES), lambda b, p: (b, blk + p, 0, 0))

    return pl.pallas_call(
        _att_kernel,
        out_shape=jax.ShapeDtypeStruct((B, ATT_WIDTH // LANES, S, LANES), F32),
        grid=(B, n_pairs),
        in_specs=[col(ATT_Q_BLK), col(ATT_K_BLK), col(ATT_V_BLK)],
        out_specs=pl.BlockSpec((None, 1, S, LANES), lambda b, p: (b, p, 0, 0)),
        scratch_shapes=[
            pltpu.VMEM((n_br, 3, 2 * N_SIDE, ATT_KEYS), F32),
            pltpu.VMEM((n_br, S, LANES), BF16),
            pltpu.VMEM((n_br, S, LANES), BF16),
            pltpu.VMEM((n_br, S, LANES), BF16),
            pltpu.VMEM((S, LANES), F32),
            pltpu.VMEM((n_br, S, LANES), F32),
            pltpu.VMEM((n_br, S, LANES), F32),
            pltpu.VMEM((n_br, S, LANES), F32),
            pltpu.VMEM((2, ATT_UNROLL, 2 * N_SIDE, ATT_KEYS), F32),
            pltpu.VMEM((2, ATT_UNROLL, 2 * N_SIDE, ATT_KEYS), BF16),
            pltpu.VMEM((2, ATT_UNROLL, 2 * N_SIDE, LANES), F32),
        ],
        compiler_params=pltpu.CompilerParams(
            dimension_semantics=("arbitrary", "arbitrary"), vmem_limit_bytes=VMEM_LIMIT),
        name="dilated_attention",
    )(proj, proj, proj)


def _ret_kernel(q_ref, k_ref, v_ref, g_ref, gn_ref, o_ref, sf_ref, sb_ref):
    S = q_ref.shape[1]
    C = RET_CHUNK
    n_chunks = S // C
    pair = pl.program_id(1)
    lane = lax.broadcasted_iota(jnp.int32, (1, LANES), 1)
    in_head = (lane < HEAD_DIM, lane >= HEAD_DIM)
    block = _head_block_mask((LANES, LANES))
    hsum = _head_sum_matrix()

    def log_gamma(head_f32):
        return jnp.log1p(-jnp.exp2(-5.0 - head_f32))

    zeros_row = jnp.zeros((1, LANES), F32)
    lg_head = [log_gamma((2 * pair + h).astype(F32) + zeros_row) for h in range(2)]
    lg = jnp.where(in_head[0], lg_head[0], lg_head[1])
    pos = lax.broadcasted_iota(jnp.int32, (C, LANES), 0).astype(F32)
    k_fwd = jnp.exp(lg * (C - 1.0 - pos))
    k_bwd = jnp.exp(lg * pos)
    q_fwd = jnp.exp(lg * (pos + 1.0))
    q_bwd = jnp.exp(lg * (C - pos))
    state_decay = jnp.where(block, jnp.exp(lg * C), 0.0)
    ri = lax.broadcasted_iota(jnp.int32, (C, C), 0)
    ci = lax.broadcasted_iota(jnp.int32, (C, C), 1)
    dist = jnp.abs(ri - ci).astype(F32)
    intra = [jnp.exp(lg_head[h] * dist) for h in range(2)]

    def chunk_rows(n):
        return pl.ds(pl.multiple_of(n * C, C), C)

    def incr_body(it, carry):
        for u in range(RET_UNROLL):
            n = it * RET_UNROLL + u
            rows = chunk_rows(n)
            k = k_ref[0, rows, :] * (HEAD_DIM ** -0.5)
            v = v_ref[0, rows, :]
            sf_ref[n] = jnp.where(block, _dot(k * k_fwd, v, TN_DIMS), 0.0)
            sb_ref[n] = jnp.where(block, _dot(k * k_bwd, v, TN_DIMS), 0.0)
        return carry

    lax.fori_loop(0, n_chunks // RET_UNROLL, incr_body, 0)

    def scan_fwd(n, state):
        incr = sf_ref[n]
        sf_ref[n] = state
        return state_decay * state + incr

    def scan_bwd(i, state):
        n = n_chunks - 1 - i
        incr = sb_ref[n]
        sb_ref[n] = state
        return state_decay * state + incr

    lax.fori_loop(0, n_chunks, scan_fwd, jnp.zeros((LANES, LANES), F32), unroll=4)
    lax.fori_loop(0, n_chunks, scan_bwd, jnp.zeros((LANES, LANES), F32), unroll=4)

    gn = gn_ref[...]

    def out_body(it, carry):
        rows = [chunk_rows(it * RET_UNROLL + u) for u in range(RET_UNROLL)]
        qs = [q_ref[0, r, :] for r in rows]
        vs = [v_ref[0, r, :] for r in rows]
        scores = []
        for u in range(RET_UNROLL):
            k = k_ref[0, rows[u], :] * (HEAD_DIM ** -0.5)
            scores.append([_dot(jnp.where(in_head[h], qs[u], 0.0), k, NT_DIMS) for h in range(2)])
        outs = []
        for u in range(RET_UNROLL):
            n = it * RET_UNROLL + u
            lhs = jnp.concatenate([(qs[u] * q_fwd).astype(BF16), (qs[u] * q_bwd).astype(BF16)], axis=1)
            rhs = jnp.concatenate([sf_ref[n].astype(BF16), sb_ref[n].astype(BF16)], axis=0)
            outs.append(_dot(lhs, rhs))
        for u in range(RET_UNROLL):
            lhs = jnp.concatenate([(scores[u][h] * intra[h]).astype(BF16) for h in range(2)], axis=1)
            rhs = jnp.concatenate([jnp.where(in_head[h], vs[u], 0.0).astype(BF16) for h in range(2)], axis=0)
            outs[u] = outs[u] + _dot(lhs, rhs)
        hsum2 = jnp.concatenate([hsum, hsum], axis=0)
        mus = []
        for u in range(RET_UNROLL):
            mus.append(_dot(jnp.concatenate(_split2(outs[u]), axis=1), hsum2) * (1.0 / HEAD_DIM))
        cens = [outs[u] - mus[u] for u in range(RET_UNROLL)]
        for u in range(RET_UNROLL):
            var = _dot(jnp.concatenate(_split2(cens[u] * cens[u]), axis=1), hsum2) * (1.0 / HEAD_DIM)
            g = g_ref[0, rows[u], :]
            o_ref[0, rows[u], :] = cens[u] * lax.rsqrt(var + NORM_EPS) * gn * (g * _sigmoid(g))
        return carry

    lax.fori_loop(0, n_chunks // RET_UNROLL, out_body, 0)


def _retention(proj, ret_gn):
    B, _, S, _ = proj.shape
    n_pairs = RET_WIDTH // LANES

    def col(blk):
        return pl.BlockSpec((None, 1, S, LANES), lambda b, p: (b, blk + p, 0, 0))

    return pl.pallas_call(
        _ret_kernel,
        out_shape=jax.ShapeDtypeStruct((B, RET_WIDTH // LANES, S, LANES), F32),
        grid=(B, n_pairs),
        in_specs=[col(RET_Q_BLK), col(RET_K_BLK), col(RET_V_BLK), col(RET_G_BLK),
                  pl.BlockSpec((1, LANES), lambda b, p: (0, p))],
        out_specs=pl.BlockSpec((None, 1, S, LANES), lambda b, p: (b, p, 0, 0)),
        scratch_shapes=[pltpu.VMEM((S // RET_CHUNK, LANES, LANES), F32),
                        pltpu.VMEM((S // RET_CHUNK, LANES, LANES), F32)],
        compiler_params=pltpu.CompilerParams(
            dimension_semantics=("arbitrary", "arbitrary"), vmem_limit_bytes=VMEM_LIMIT),
        name="retention",
    )(proj, proj, proj, proj, ret_gn.reshape(1, RET_WIDTH))


HGRN_TILE = 128
HGRN_SLABS = HGRN_TILE * (HGRN_CHUNK // SUBLANES) * 3 // 4
HGRN_SLOTS = 4


def _hgrn_kernel(layer, q_ref, zf_ref, zb_ref, i_ref, lbl_ref, o_ref,
                 qt_s, b_s, c_s, v_s, dec_s, qe_s, ke_s, w_s, mask_s):
    S = q_ref.shape[1]
    T, C = HGRN_TILE, HGRN_CHUNK
    n_tiles = S // T
    chunks = T // C
    block = _head_block_mask((LANES, LANES))
    hsum = _head_sum_matrix()

    logits = [lbl_ref[l:l + 1, :] for l in range(DEPTH)]
    mx = functools.reduce(jnp.maximum, logits)
    ex = [jnp.exp(t - mx) for t in logits]
    den = functools.reduce(lambda a, b: a + b, ex)
    probs = [e / den for e in ex]
    cum = functools.reduce(lambda a, b: a + b, probs[:layer + 1])
    lb = jnp.clip(cum - probs[0], 0.0, 1.0 - 1e-6)
    lb_floor = jnp.maximum(lb, LB_FLOOR)
    one_minus_lb = 1.0 - lb

    ri = lax.broadcasted_iota(jnp.int32, (T, T), 0)
    ci = lax.broadcasted_iota(jnp.int32, (T, T), 1)
    chunk_bits = C.bit_length() - 1
    same_chunk = jnp.right_shift(ri, chunk_bits) == jnp.right_shift(ci, chunk_bits)
    tri = (jnp.where(same_chunk & (ci <= ri), 1.0, 0.0).astype(BF16),
           jnp.where(same_chunk & (ci >= ri), 1.0, 0.0).astype(BF16))
    ones_chunk = jnp.where(same_chunk, 1.0, 0.0).astype(BF16)

    def exact_sum(mat, parts):
        return _dot(jnp.concatenate([mat] * len(parts), axis=1), jnp.concatenate(parts, axis=0))

    t_loc = lax.broadcasted_iota(jnp.int32, (SUBLANES, LANES), 0)
    for s in range(SUBLANES):
        mask_s[0, s] = jnp.where(t_loc >= s, 0.0, NEG_INF)
        mask_s[1, s] = jnp.where(t_loc <= s, 0.0, NEG_INF)

    def slab_list(backward):
        slabs = []
        for c in range(chunks):
            for s in range(C):
                for half in range(C // SUBLANES):
                    if (half <= s // SUBLANES) if backward else (half >= s // SUBLANES):
                        slabs.append((c, s, half))
        return slabs

    slabs_of = {False: slab_list(False), True: slab_list(True)}

    def tile_rows(step, backward):
        tile = (n_tiles - 1 - step) if backward else step
        if isinstance(tile, int):
            return pl.ds(tile * T, T)
        return pl.ds(pl.multiple_of(tile * T, T), T)

    def prepare(step, backward, slot):
        rows = tile_rows(step, backward)
        z = (zb_ref if backward else zf_ref)[0, rows, :]
        hq = q_ref[0, rows, :]
        q = hq * _sigmoid(hq)
        e = jnp.exp(-jnp.abs(z))
        r = 1.0 / (1.0 + e)
        er = e * r
        nonneg = z >= 0.0
        f = lb_floor + one_minus_lb * jnp.where(nonneg, r, er)
        kk = one_minus_lb * jnp.where(nonneg, er, r)
        parts = _split3(jnp.log(f))
        b2 = exact_sum(tri[1] if backward else tri[0], parts) * LOG2E
        b2_tot = exact_sum(ones_chunk, parts) * LOG2E
        qt_s[slot] = q
        b_s[slot] = b2
        c_s[slot] = b2 - jnp.log(kk) * LOG2E
        v_s[slot] = i_ref[0, rows, :]
        qe_s[slot] = (q * jnp.exp2(b2)).astype(BF16)
        ke_s[slot] = (kk * jnp.exp2(b2_tot - b2)).astype(BF16)
        dec_s[slot] = jnp.exp2(b2_tot)

    def advance(backward, slot, state):
        slabs = slabs_of[backward]
        per_chunk = len(slabs) // chunks
        attns = []
        for c in range(chunks):
            base = c * per_chunk
            for i in range(base, base + per_chunk, 2):
                ws = []
                for cc, s, half in slabs[i:i + 2]:
                    lo = cc * C + half * SUBLANES
                    c_src = jnp.broadcast_to(c_s[slot, cc * C + s:cc * C + s + 1, :], (SUBLANES, LANES))
                    expo = b_s[slot, lo:lo + SUBLANES, :] - c_src
                    if half == s // SUBLANES:
                        expo = expo + mask_s[int(backward), s % SUBLANES]
                    ws.append(qt_s[slot, lo:lo + SUBLANES, :] * jnp.exp2(expo))
                w_s[slot, i * SUBLANES:(i + 2) * SUBLANES, :] = jnp.concatenate(ws, axis=0).astype(BF16)
            attns.append(_dot(w_s[slot, base * SUBLANES:(base + per_chunk) * SUBLANES, :], hsum))

        order = range(chunks - 1, -1, -1) if backward else range(chunks)
        upds = {c: _dot(v_s[slot, c * C:(c + 1) * C, :].astype(BF16), ke_s[slot, c * C:(c + 1) * C, :], TN_DIMS)
                for c in order}
        states = {}
        for c in order:
            states[c] = state
            state = state * dec_s[slot, c * C:c * C + 1, :] + jnp.where(block, upds[c], 0.0)
        outs = {c: _dot(qe_s[slot, c * C:(c + 1) * C, :], states[c].astype(BF16), NT_DIMS) for c in order}

        pieces = []
        for c in range(chunks):
            acc = [None] * (C // SUBLANES)
            for i in range(per_chunk):
                cc, s, half = slabs[c * per_chunk + i]
                v_src = jnp.broadcast_to(v_s[slot, cc * C + s:cc * C + s + 1, :], (SUBLANES, LANES))
                term = attns[c][i * SUBLANES:(i + 1) * SUBLANES] * v_src
                acc[half] = term if acc[half] is None else acc[half] + term
            pieces.extend(acc[h] + outs[c][h * SUBLANES:(h + 1) * SUBLANES] for h in range(C // SUBLANES))
        return jnp.concatenate(pieces, axis=0), state

    for backward in (False, True):
        def trip(i, state, backward=backward):
            base = 4 * i
            outs = []
            for slot in (0, 1):
                out, state = advance(backward, slot, state)
                outs.append(out)
            prepare(base + 2, backward, 2)
            prepare(base + 3, backward, 3)
            for slot in (2, 3):
                out, state = advance(backward, slot, state)
                outs.append(out)
            prepare(jnp.minimum(base + 4, n_tiles - 1), backward, 0)
            prepare(jnp.minimum(base + 5, n_tiles - 1), backward, 1)
            for k, out in enumerate(outs):
                rows = tile_rows(base + k, backward)
                o_ref[0, rows, :] = (o_ref[0, rows, :] + out) if backward else out
            return state

        prepare(0, backward, 0)
        prepare(1, backward, 1)
        lax.fori_loop(0, n_tiles // 4, trip, jnp.zeros((LANES, LANES), F32))


def _hgrn(proj, lb_logits, layer):
    B, _, S, _ = proj.shape
    n_pairs = HGRN_WIDTH // LANES

    def col(blk):
        return pl.BlockSpec((None, 1, S, LANES), lambda b, p: (b, blk + p, 0, 0))

    return pl.pallas_call(
        functools.partial(_hgrn_kernel, layer),
        out_shape=jax.ShapeDtypeStruct((B, HGRN_WIDTH // LANES, S, LANES), F32),
        grid=(B, n_pairs),
        in_specs=[col(HG_Q_BLK), col(HG_ZF_BLK), col(HG_ZB_BLK), col(HG_I_BLK),
                  pl.BlockSpec((DEPTH, LANES), lambda b, p: (0, p))],
        out_specs=pl.BlockSpec((None, 1, S, LANES), lambda b, p: (b, p, 0, 0)),
        scratch_shapes=(
            [pltpu.VMEM((HGRN_SLOTS, HGRN_TILE, LANES), F32)] * 5
            + [pltpu.VMEM((HGRN_SLOTS, HGRN_TILE, LANES), BF16)] * 2
            + [pltpu.VMEM((HGRN_SLOTS, HGRN_SLABS * SUBLANES, LANES), BF16)]
            + [pltpu.VMEM((2, SUBLANES, SUBLANES, LANES), F32)]),
        compiler_params=pltpu.CompilerParams(
            dimension_semantics=("arbitrary", "arbitrary"), vmem_limit_bytes=VMEM_LIMIT),
        name="hgrn2",
    )(proj, proj, proj, proj, lb_logits)


def _ffn_kernel(final_norm, att_ref, ret_ref, hg_ref, hga_ref, hgb_ref, x_ref, wo_ref, hgn_ref,
                gate1_ref, gffn_ref, sc2_ref, sh2_ref, gate2_ref, wgu_ref, wd_ref, gfin_ref,
                o_ref, x2_s, h_s, acc_s):
    def wide(ref):
        return jnp.concatenate([ref[0, j] for j in range(ref.shape[1])], axis=-1)

    hraw = wide(hg_ref)
    hgate = jnp.concatenate([hga_ref[0], hgb_ref[0]], axis=-1)
    hgo = (hraw * lax.rsqrt(jnp.mean(hraw * hraw, axis=-1, keepdims=True) + NORM_EPS)
           * hgn_ref[...] * (hgate * _sigmoid(hgate)))
    mixed = jnp.concatenate([wide(att_ref).astype(BF16), wide(ret_ref).astype(BF16), hgo.astype(BF16)], axis=-1)
    y = _dot(mixed, wo_ref[...])
    x2 = x_ref[0] + gate1_ref[0] * y
    x2_s[...] = x2
    n = x2 * lax.rsqrt(jnp.mean(x2 * x2, axis=-1, keepdims=True) + NORM_EPS) * gffn_ref[...]
    h_s[...] = (n * (1.0 + sc2_ref[0]) + sh2_ref[0]).astype(BF16)

    lo = 0
    for width in FFN_CHUNKS:
        h = h_s[...]
        g = _dot(h, wgu_ref[:, lo:lo + width])
        u = _dot(h, wgu_ref[:, FFN_HIDDEN + lo:FFN_HIDDEN + lo + width])
        a = (g * _sigmoid(g) * u).astype(BF16)
        down = _dot(a, wd_ref[lo:lo + width, :])
        acc_s[...] = down if lo == 0 else acc_s[...] + down
        lo += width

    out = x2_s[...] + gate2_ref[0] * acc_s[...]
    if final_norm:
        out = out * lax.rsqrt(jnp.mean(out * out, axis=-1, keepdims=True) + NORM_EPS) * gfin_ref[...]
    o_ref[0] = out


def _out_proj_ffn(att, ret, hg_raw, proj, x, wo, hgrn_gn, gate1, g_ffn, scale2, shift2, gate2,
                  w_gate_up, w_down, g_final, final_norm):
    B, S, D = x.shape
    tm = 512

    def rows(width):
        return pl.BlockSpec((1, tm, width), lambda b, i: (b, i, 0))

    def blocks(width):
        return pl.BlockSpec((1, width // LANES, tm, LANES), lambda b, i: (b, 0, i, 0))

    def per_batch():
        return pl.BlockSpec((1, 1, D), lambda b, i: (b, 0, 0))

    def vec(width):
        return pl.BlockSpec((1, width), lambda b, i: (0, 0))

    def resident(shape):
        return pl.BlockSpec(shape, lambda b, i: (0, 0), pipeline_mode=pl.Buffered(1))

    return pl.pallas_call(
        functools.partial(_ffn_kernel, final_norm),
        out_shape=jax.ShapeDtypeStruct((B, S, D), F32),
        grid=(B, S // tm),
        in_specs=[
            blocks(ATT_WIDTH), blocks(RET_WIDTH), blocks(HGRN_WIDTH),
            pl.BlockSpec((None, 1, tm, LANES), lambda b, i: (b, HG_G_BLK, i, 0)),
            pl.BlockSpec((None, 1, tm, LANES), lambda b, i: (b, HG_G_BLK + 1, i, 0)),
            rows(D),
            resident((D, D)),
            vec(HGRN_WIDTH),
            per_batch(), vec(D), per_batch(), per_batch(), per_batch(),
            resident((D, 2 * FFN_HIDDEN)),
            resident((FFN_HIDDEN, D)),
            vec(D),
        ],
        out_specs=pl.BlockSpec((1, tm, D), lambda b, i: (b, i, 0)),
        scratch_shapes=[
            pltpu.VMEM((tm, D), F32),
            pltpu.VMEM((tm, D), BF16),
            pltpu.VMEM((tm, D), F32),
        ],
        compiler_params=pltpu.CompilerParams(
            dimension_semantics=("arbitrary", "arbitrary"), vmem_limit_bytes=VMEM_LIMIT),
        name="out_proj_ffn",
    )(att, ret, hg_raw, proj, proj, x, wo, hgrn_gn.reshape(1, HGRN_WIDTH), gate1,
      g_ffn.reshape(1, D), scale2, shift2, gate2, w_gate_up, w_down, g_final.reshape(1, D))


def kernel(x, c, w_ada, b_ada, g_mix, w_in, ret_gn, hgrn_gn, hgrn_lb_logits, w_out,
           g_ffn, w_gate_up, w_down, g_final):
    B, S, D = x.shape
    mod = _modulation(c, w_ada, b_ada).reshape(DEPTH, B, N_MOD, 1, D)
    for layer in range(DEPTH):
        shift1, scale1, gate1, shift2, scale2, gate2 = (mod[layer, :, i] for i in range(N_MOD))
        proj = _in_projection(x, g_mix[layer], scale1, shift1, w_in[layer].astype(BF16))
        att = _attention(proj)
        ret = _retention(proj, ret_gn[layer])
        hg_raw = _hgrn(proj, hgrn_lb_logits, layer)
        x = _out_proj_ffn(att, ret, hg_raw, proj, x, w_out[layer].astype(BF16), hgrn_gn[layer],
                          gate1, g_ffn[layer], scale2, shift2, gate2,
                          w_gate_up[layer].astype(BF16), w_down[layer].astype(BF16), g_final,
                          final_norm=(layer == DEPTH - 1))
    return x
```

```python
import functools

import jax
import jax.numpy as jnp
from jax import lax
from jax.experimental import pallas as pl
from jax.experimental.pallas import tpu as pltpu

D_MODEL = 1024
DEPTH = 2
HEAD_DIM = 64
ATT_WIDTH = 384
RET_WIDTH = 384
HGRN_WIDTH = 256
IN_WIDTH = 3 * ATT_WIDTH + 4 * RET_WIDTH + 5 * HGRN_WIDTH
DILATED_CONFIGS = ((128, 1), (512, 4), (2048, 16))
N_SIDE = 64
RET_CHUNK = 128
HGRN_CHUNK = 16
FFN_HIDDEN = 2816
FFN_CHUNKS = (512, 512, 512, 512, 512, 256)
N_MOD = 6
NORM_EPS = 1e-6
NEG_INF = -1e30
LB_FLOOR = 1e-30

LANES = 128
SUBLANES = 8
LOG2E = 1.4426950408889634
ATT_Q_BLK, ATT_K_BLK, ATT_V_BLK = 0, 3, 6
RET_Q_BLK, RET_K_BLK, RET_V_BLK, RET_G_BLK = 9, 12, 15, 18
HG_Q_BLK, HG_ZF_BLK, HG_ZB_BLK, HG_I_BLK, HG_G_BLK = 21, 23, 25, 27, 29

VMEM_LIMIT = 56 * 1024 * 1024

F32 = jnp.float32
BF16 = jnp.bfloat16

NT_DIMS = (((1,), (1,)), ((), ()))
TN_DIMS = (((0,), (0,)), ((), ()))


def _dot(a, b, dims=None):
    if dims is None:
        return jnp.dot(a, b, preferred_element_type=F32)
    return lax.dot_general(a, b, dims, preferred_element_type=F32)


def _split2(x):
    hi = x.astype(BF16)
    lo = (x - hi.astype(F32)).astype(BF16)
    return hi, lo


def _split3(x):
    hi = x.astype(BF16)
    r1 = x - hi.astype(F32)
    mid = r1.astype(BF16)
    lo = (r1 - mid.astype(F32)).astype(BF16)
    return hi, mid, lo


def _head_block_mask(shape):
    r = lax.broadcasted_iota(jnp.int32, shape, 0)
    c = lax.broadcasted_iota(jnp.int32, shape, 1)
    return (r >= HEAD_DIM) == (c >= HEAD_DIM)


def _head_sum_matrix():
    return jnp.where(_head_block_mask((LANES, LANES)), 1.0, 0.0).astype(BF16)


def _sigmoid(x):
    return 1.0 / (1.0 + jnp.exp(-x))


def _mod_kernel(c_ref, w_ref, b_ref, o_ref):
    c = c_ref[...]
    cond = c * _sigmoid(c)
    o_ref[0] = jnp.dot(cond, w_ref[0], preferred_element_type=F32,
                       precision=lax.Precision.HIGHEST) + b_ref[0]


def _modulation(c, w_ada, b_ada):
    B, D = c.shape
    n_out = w_ada.shape[-1]
    tn = 1536
    return pl.pallas_call(
        _mod_kernel,
        out_shape=jax.ShapeDtypeStruct((DEPTH, B, n_out), F32),
        grid=(DEPTH, n_out // tn),
        in_specs=[
            pl.BlockSpec((B, D), lambda l, j: (0, 0)),
            pl.BlockSpec((1, D, tn), lambda l, j: (l, 0, j)),
            pl.BlockSpec((1, 1, tn), lambda l, j: (l, 0, j)),
        ],
        out_specs=pl.BlockSpec((1, B, tn), lambda l, j: (l, 0, j)),
        compiler_params=pltpu.CompilerParams(
            dimension_semantics=("arbitrary", "arbitrary"), vmem_limit_bytes=VMEM_LIMIT),
        name="adaln_mod",
    )(c, w_ada, b_ada.reshape(DEPTH, 1, n_out))


def _inproj_kernel(x_ref, g_ref, sc_ref, sh_ref, w_ref, o_ref):
    x = x_ref[0]
    y = x * lax.rsqrt(jnp.mean(x * x, axis=-1, keepdims=True) + NORM_EPS) * g_ref[...]
    h = y * (1.0 + sc_ref[0]) + sh_ref[0]
    res = _dot(h.astype(BF16), w_ref[...])
    for j in range(o_ref.shape[1]):
        o_ref[0, j] = res[:, j * LANES:(j + 1) * LANES]


def _in_projection(x, gain, scale, shift, w_bf16):
    B, S, D = x.shape
    N = w_bf16.shape[-1]
    tm = 512
    return pl.pallas_call(
        _inproj_kernel,
        out_shape=jax.ShapeDtypeStruct((B, N // LANES, S, LANES), F32),
        grid=(B, S // tm),
        in_specs=[
            pl.BlockSpec((1, tm, D), lambda b, i: (b, i, 0)),
            pl.BlockSpec((1, D), lambda b, i: (0, 0)),
            pl.BlockSpec((1, 1, D), lambda b, i: (b, 0, 0)),
            pl.BlockSpec((1, 1, D), lambda b, i: (b, 0, 0)),
            pl.BlockSpec((D, N), lambda b, i: (0, 0), pipeline_mode=pl.Buffered(1)),
        ],
        out_specs=pl.BlockSpec((1, N // LANES, tm, LANES), lambda b, i: (b, 0, i, 0)),
        compiler_params=pltpu.CompilerParams(
            dimension_semantics=("arbitrary", "arbitrary"), vmem_limit_bytes=VMEM_LIMIT),
        name="in_proj",
    )(x, gain.reshape(1, D), scale, shift, w_bf16)


ATT_KEYS = 3 * N_SIDE
ATT_UNROLL = 8
RET_UNROLL = 8


def _att_kernel(q_ref, k_ref, v_ref, o_ref, bias_ref, qd_ref, kd_ref, vd_ref, t4_ref, ob_ref, mb_ref, lb_ref,
                s_ref, p_ref, m_ref):
    S = q_ref.shape[1]
    pair = pl.program_id(0)
    lane = lax.broadcasted_iota(jnp.int32, (1, LANES), 1)
    in_head = (lane < HEAD_DIM, lane >= HEAD_DIM)
    lane_q = lax.broadcasted_iota(jnp.int32, (N_SIDE, LANES), 1)
    q_in_head = (lane_q < HEAD_DIM, lane_q >= HEAD_DIM)

    @pl.when(pl.program_id(1) == 0)
    def _():
        qi = lax.broadcasted_iota(jnp.int32, (N_SIDE, ATT_KEYS), 0)
        kj = lax.broadcasted_iota(jnp.int32, (N_SIDE, ATT_KEYS), 1)
        for br, (_, dil) in enumerate(DILATED_CONFIGS):
            for case in range(3):
                steps = jnp.abs(qi + N_SIDE * case - kj)
                dist = (dil * steps).astype(F32)
                for h in range(2):
                    head = (2 * pair + h + 1).astype(F32) + jnp.zeros((1, ATT_KEYS), F32)
                    slope = jnp.exp2(-8.0 * head / (ATT_WIDTH // HEAD_DIM)) * LOG2E
                    bias_ref[br, case, h * N_SIDE:(h + 1) * N_SIDE, :] = jnp.where(
                        steps <= N_SIDE, -slope * dist, NEG_INF)

    assert [d for _, d in DILATED_CONFIGS] == [1, 4, 16]
    piece = 256
    wide = 4
    quarter = S // 4
    assert S // 16 == piece and (S // piece) % wide == 0
    for src_ref, dst_ref, scale in ((q_ref, qd_ref, HEAD_DIM ** -0.5 * LOG2E), (k_ref, kd_ref, 1.0),
                                    (v_ref, vd_ref, 1.0)):
        def pass1(t, carry, src_ref=src_ref, dst_ref=dst_ref, scale=scale):
            dsts, plain, strided = [], [], []
            for w in range(wide):
                j = t * wide + w
                dst = pl.ds(pl.multiple_of(j * piece, piece), piece)
                cls = j // (quarter // piece)
                first = (j % (quarter // piece)) * piece
                dsts.append(dst)
                plain.append(src_ref[0, dst, :] * scale)
                strided.append(src_ref[0, pl.ds(cls + 4 * first, piece, stride=4), :] * scale)
            for dst, x0, x4 in zip(dsts, plain, strided):
                dst_ref[0, dst, :] = x0.astype(BF16)
                t4_ref[dst, :] = x4
                dst_ref[1, dst, :] = x4.astype(BF16)
            return carry

        lax.fori_loop(0, S // piece // wide, pass1, 0)

        def pass2(t, carry, dst_ref=dst_ref):
            xs = []
            for w in range(wide):
                j = t * wide + w
                xs.append(t4_ref[pl.ds((j % 4) * quarter + j // 4, piece, stride=4), :])
            for w, x in enumerate(xs):
                dst = pl.ds(pl.multiple_of((t * wide + w) * piece, piece), piece)
                dst_ref[2, dst, :] = x.astype(BF16)
            return carry

        lax.fori_loop(0, S // piece // wide, pass2, 0)

    zero_q = jnp.zeros((N_SIDE, LANES), BF16)
    ones_v = jnp.ones((ATT_KEYS, LANES), BF16)
    branch_stages = []
    for br, (_, dil) in enumerate(DILATED_CONFIGS):
        sub_len = S // dil
        n_blocks = sub_len // N_SIDE
        blk_bits = n_blocks.bit_length() - 1

        n_groups = dil * n_blocks // ATT_UNROLL

        def block_index(g, u, n_blocks=n_blocks, blk_bits=blk_bits, sub_len=sub_len):
            idx = g * ATT_UNROLL + u
            c = jnp.right_shift(idx, blk_bits)
            n = jnp.bitwise_and(idx, n_blocks - 1)
            win = jnp.clip(N_SIDE * n - N_SIDE, 0, sub_len - ATT_KEYS)
            case = jnp.right_shift(N_SIDE * n - win, 6)
            q_src = pl.ds(pl.multiple_of(N_SIDE * idx, N_SIDE), N_SIDE)
            k_src = pl.ds(pl.multiple_of(N_SIDE * (idx - n) + win, N_SIDE), ATT_KEYS)
            return c, n, case, q_src, k_src

        def scores_stage(g, slot, br=br, block_index=block_index):
            for u in range(ATT_UNROLL):
                _, _, case, q_src, k_src = block_index(g, u)
                q = qd_ref[br, q_src, :]
                q2 = jnp.concatenate([jnp.where(q_in_head[h], q, zero_q) for h in range(2)], axis=0)
                s_ref[slot, u] = _dot(q2, kd_ref[br, k_src, :], NT_DIMS) + bias_ref[br, case]

        def values_stage(g, slot, br=br, dil=dil, block_index=block_index):
            for u in range(ATT_UNROLL):
                c, n, _, q_src, k_src = block_index(g, u)
                rows = q_src if dil == 1 else pl.ds(c + dil * N_SIDE * n, N_SIDE, stride=dil)
                ol = _dot(p_ref[slot, u], jnp.concatenate([vd_ref[br, k_src, :], ones_v], axis=1))
                o, l = ol[:, :LANES], ol[:, LANES:]
                m = m_ref[slot, u]
                ob_ref[br, rows, :] = jnp.where(in_head[0], o[:N_SIDE], o[N_SIDE:])
                mb_ref[br, rows, :] = jnp.where(in_head[0], m[:N_SIDE], m[N_SIDE:])
                lb_ref[br, rows, :] = jnp.where(in_head[0], l[:N_SIDE], l[N_SIDE:])

        branch_stages.append((scores_stage, values_stage, n_groups))

    def softmax_stage(slot):
        for u in range(ATT_UNROLL):
            s = s_ref[slot, u]
            m = jnp.max(s, axis=-1, keepdims=True)
            p = jnp.exp2(s - m)
            p_ref[slot, u] = p.astype(BF16)
            m_ref[slot, u] = jnp.broadcast_to(m, (2 * N_SIDE, LANES))

    def tick(stage1, stage2_slot, stage3):
        if stage3 is not None:
            stage3[0](stage3[1], stage3[2])
        if stage2_slot is not None:
            softmax_stage(stage2_slot)
        if stage1 is not None:
            stage1[0](stage1[1], stage1[2])

    prev = None
    for scores_stage, values_stage, n_groups in branch_stages:
        assert n_groups % 2 == 0
        for par in range(2):
            older = None if prev is None else (prev[0], prev[1] - 2 + par, par)
            middle = (1 - par) if (prev is not None or par == 1) else None
            tick((scores_stage, par, par), middle, older)

        def trip(i, carry, scores_stage=scores_stage, values_stage=values_stage):
            for par in range(2):
                g = 2 * i + par
                tick((scores_stage, g, par), 1 - par, (values_stage, g - 2, par))
            return carry

        lax.fori_loop(1, n_groups // 2, trip, 0)
        prev = (values_stage, n_groups)
    tick(None, 1, (prev[0], prev[1] - 2, 0))
    tick(None, None, (prev[0], prev[1] - 1, 1))

    tile = 512

    def merge_body(t, carry):
        rows = pl.ds(pl.multiple_of(t * tile, tile), tile)
        ms = [mb_ref[b, rows, :] for b in range(len(DILATED_CONFIGS))]
        top = functools.reduce(jnp.maximum, ms)
        es = [jnp.exp2(m - top) for m in ms]
        num = functools.reduce(lambda a, b: a + b, [e * ob_ref[b, rows, :] for b, e in enumerate(es)])
        den = functools.reduce(lambda a, b: a + b, [e * lb_ref[b, rows, :] for b, e in enumerate(es)])
        o_ref[0, rows, :] = num / den
        return carry

    lax.fori_loop(0, S // tile, merge_body, 0)


def _attention(proj):
    B, _, S, _ = proj.shape
    n_pairs = ATT_WIDTH // LANES
    n_br = len(DILATED_CONFIGS)

    def col(blk):
        return pl.BlockSpec((None, 1, S, LANES), lambda p, b: (b, blk + p, 0, 0))

    return pl.pallas_call(
        _att_kernel,
        out_shape=jax.ShapeDtypeStruct((B, ATT_WIDTH // LANES, S, LANES), F32),
        grid=(n_pairs, B),
        in_specs=[col(ATT_Q_BLK), col(ATT_K_BLK), col(ATT_V_BLK)],
        out_specs=pl.BlockSpec((None, 1, S, LANES), lambda p, b: (b, p, 0, 0)),
        scratch_shapes=[
            pltpu.VMEM((n_br, 3, 2 * N_SIDE, ATT_KEYS), F32),
            pltpu.VMEM((n_br, S, LANES), BF16),
            pltpu.VMEM((n_br, S, LANES), BF16),
            pltpu.VMEM((n_br, S, LANES), BF16),
            pltpu.VMEM((S, LANES), F32),
            pltpu.VMEM((n_br, S, LANES), F32),
            pltpu.VMEM((n_br, S, LANES), F32),
            pltpu.VMEM((n_br, S, LANES), F32),
            pltpu.VMEM((2, ATT_UNROLL, 2 * N_SIDE, ATT_KEYS), F32),
            pltpu.VMEM((2, ATT_UNROLL, 2 * N_SIDE, ATT_KEYS), BF16),
            pltpu.VMEM((2, ATT_UNROLL, 2 * N_SIDE, LANES), F32),
        ],
        compiler_params=pltpu.CompilerParams(
            dimension_semantics=("arbitrary", "arbitrary"), vmem_limit_bytes=VMEM_LIMIT),
        name="dilated_attention",
    )(proj, proj, proj)


def _ret_kernel(q_ref, k_ref, v_ref, g_ref, gn_ref, o_ref, sf_ref, sb_ref):
    S = q_ref.shape[1]
    C = RET_CHUNK
    n_chunks = S // C
    pair = pl.program_id(1)
    lane = lax.broadcasted_iota(jnp.int32, (1, LANES), 1)
    in_head = (lane < HEAD_DIM, lane >= HEAD_DIM)
    block = _head_block_mask((LANES, LANES))
    hsum = _head_sum_matrix()

    def log_gamma(head_f32):
        return jnp.log1p(-jnp.exp2(-5.0 - head_f32))

    zeros_row = jnp.zeros((1, LANES), F32)
    lg_head = [log_gamma((2 * pair + h).astype(F32) + zeros_row) for h in range(2)]
    lg = jnp.where(in_head[0], lg_head[0], lg_head[1])
    pos = lax.broadcasted_iota(jnp.int32, (C, LANES), 0).astype(F32)
    k_fwd = jnp.exp(lg * (C - 1.0 - pos))
    k_bwd = jnp.exp(lg * pos)
    q_fwd = jnp.exp(lg * (pos + 1.0))
    q_bwd = jnp.exp(lg * (C - pos))
    state_decay = jnp.where(block, jnp.exp(lg * C), 0.0)
    ri = lax.broadcasted_iota(jnp.int32, (C, C), 0)
    ci = lax.broadcasted_iota(jnp.int32, (C, C), 1)
    dist = jnp.abs(ri - ci).astype(F32)
    intra = [jnp.exp(lg_head[h] * dist) for h in range(2)]

    def chunk_rows(n):
        return pl.ds(pl.multiple_of(n * C, C), C)

    def incr_body(it, carry):
        for u in range(RET_UNROLL):
            n = it * RET_UNROLL + u
            rows = chunk_rows(n)
            k = k_ref[0, rows, :] * (HEAD_DIM ** -0.5)
            v = v_ref[0, rows, :]
            sf_ref[n] = jnp.where(block, _dot(k * k_fwd, v, TN_DIMS), 0.0)
            sb_ref[n] = jnp.where(block, _dot(k * k_bwd, v, TN_DIMS), 0.0)
        return carry

    lax.fori_loop(0, n_chunks // RET_UNROLL, incr_body, 0)

    def scan_fwd(n, state):
        incr = sf_ref[n]
        sf_ref[n] = state
        return state_decay * state + incr

    def scan_bwd(i, state):
        n = n_chunks - 1 - i
        incr = sb_ref[n]
        sb_ref[n] = state
        return state_decay * state + incr

    lax.fori_loop(0, n_chunks, scan_fwd, jnp.zeros((LANES, LANES), F32), unroll=4)
    lax.fori_loop(0, n_chunks, scan_bwd, jnp.zeros((LANES, LANES), F32), unroll=4)

    gn = gn_ref[...]

    def out_body(it, carry):
        rows = [chunk_rows(it * RET_UNROLL + u) for u in range(RET_UNROLL)]
        qs = [q_ref[0, r, :] for r in rows]
        vs = [v_ref[0, r, :] for r in rows]
        scores = []
        for u in range(RET_UNROLL):
            k = k_ref[0, rows[u], :] * (HEAD_DIM ** -0.5)
            scores.append([_dot(jnp.where(in_head[h], qs[u], 0.0), k, NT_DIMS) for h in range(2)])
        outs = []
        for u in range(RET_UNROLL):
            n = it * RET_UNROLL + u
            lhs = jnp.concatenate([(qs[u] * q_fwd).astype(BF16), (qs[u] * q_bwd).astype(BF16)], axis=1)
            rhs = jnp.concatenate([sf_ref[n].astype(BF16), sb_ref[n].astype(BF16)], axis=0)
            outs.append(_dot(lhs, rhs))
        for u in range(RET_UNROLL):
            lhs = jnp.concatenate([(scores[u][h] * intra[h]).astype(BF16) for h in range(2)], axis=1)
            rhs = jnp.concatenate([jnp.where(in_head[h], vs[u], 0.0).astype(BF16) for h in range(2)], axis=0)
            outs[u] = outs[u] + _dot(lhs, rhs)
        hsum2 = jnp.concatenate([hsum, hsum], axis=0)
        mus = []
        for u in range(RET_UNROLL):
            mus.append(_dot(jnp.concatenate(_split2(outs[u]), axis=1), hsum2) * (1.0 / HEAD_DIM))
        cens = [outs[u] - mus[u] for u in range(RET_UNROLL)]
        for u in range(RET_UNROLL):
            var = _dot(jnp.concatenate(_split2(cens[u] * cens[u]), axis=1), hsum2) * (1.0 / HEAD_DIM)
            g = g_ref[0, rows[u], :]
            o_ref[0, rows[u], :] = cens[u] * lax.rsqrt(var + NORM_EPS) * gn * (g * _sigmoid(g))
        return carry

    lax.fori_loop(0, n_chunks // RET_UNROLL, out_body, 0)


def _retention(proj, ret_gn):
    B, _, S, _ = proj.shape
    n_pairs = RET_WIDTH // LANES

    def col(blk):
        return pl.BlockSpec((None, 1, S, LANES), lambda b, p: (b, blk + p, 0, 0))

    return pl.pallas_call(
        _ret_kernel,
        out_shape=jax.ShapeDtypeStruct((B, RET_WIDTH // LANES, S, LANES), F32),
        grid=(B, n_pairs),
        in_specs=[col(RET_Q_BLK), col(RET_K_BLK), col(RET_V_BLK), col(RET_G_BLK),
                  pl.BlockSpec((1, LANES), lambda b, p: (0, p))],
        out_specs=pl.BlockSpec((None, 1, S, LANES), lambda b, p: (b, p, 0, 0)),
        scratch_shapes=[pltpu.VMEM((S // RET_CHUNK, LANES, LANES), F32),
                        pltpu.VMEM((S // RET_CHUNK, LANES, LANES), F32)],
        compiler_params=pltpu.CompilerParams(
            dimension_semantics=("arbitrary", "arbitrary"), vmem_limit_bytes=VMEM_LIMIT),
        name="retention",
    )(proj, proj, proj, proj, ret_gn.reshape(1, RET_WIDTH))


HGRN_TILE = 128
HGRN_SLABS = HGRN_TILE * (HGRN_CHUNK // SUBLANES) * 3 // 4
HGRN_SLOTS = 4
HGRN_TRIP_TILES = 8


def _hgrn_kernel(layer, q_ref, zf_ref, zb_ref, i_ref, lbl_ref, o_ref,
                 qt_s, b_s, c_s, v_s, dec_s, qe_s, ke_s, w_s, mask_s):
    S = q_ref.shape[1]
    T, C = HGRN_TILE, HGRN_CHUNK
    n_tiles = S // T
    chunks = T // C
    block = _head_block_mask((LANES, LANES))
    hsum = _head_sum_matrix()

    logits = [lbl_ref[l:l + 1, :] for l in range(DEPTH)]
    mx = functools.reduce(jnp.maximum, logits)
    ex = [jnp.exp(t - mx) for t in logits]
    den = functools.reduce(lambda a, b: a + b, ex)
    probs = [e / den for e in ex]
    cum = functools.reduce(lambda a, b: a + b, probs[:layer + 1])
    lb = jnp.clip(cum - probs[0], 0.0, 1.0 - 1e-6)
    lb_floor = jnp.maximum(lb, LB_FLOOR)
    one_minus_lb = 1.0 - lb

    ri = lax.broadcasted_iota(jnp.int32, (T, T), 0)
    ci = lax.broadcasted_iota(jnp.int32, (T, T), 1)
    chunk_bits = C.bit_length() - 1
    same_chunk = jnp.right_shift(ri, chunk_bits) == jnp.right_shift(ci, chunk_bits)
    tri = (jnp.where(same_chunk & (ci <= ri), 1.0, 0.0).astype(BF16),
           jnp.where(same_chunk & (ci >= ri), 1.0, 0.0).astype(BF16))
    ones_chunk = jnp.where(same_chunk, 1.0, 0.0).astype(BF16)

    def exact_sum(mat, parts):
        return _dot(jnp.concatenate([mat] * len(parts), axis=1), jnp.concatenate(parts, axis=0))

    t_loc = lax.broadcasted_iota(jnp.int32, (SUBLANES, LANES), 0)
    for s in range(SUBLANES):
        mask_s[0, s] = jnp.where(t_loc >= s, 0.0, NEG_INF)
        mask_s[1, s] = jnp.where(t_loc <= s, 0.0, NEG_INF)

    def slab_list(backward):
        slabs = []
        for c in range(chunks):
            for s in range(C):
                for half in range(C // SUBLANES):
                    if (half <= s // SUBLANES) if backward else (half >= s // SUBLANES):
                        slabs.append((c, s, half))
        return slabs

    slabs_of = {False: slab_list(False), True: slab_list(True)}

    def tile_rows(step, backward):
        tile = (n_tiles - 1 - step) if backward else step
        if isinstance(tile, int):
            return pl.ds(tile * T, T)
        return pl.ds(pl.multiple_of(tile * T, T), T)

    def prepare(step, backward, slot):
        rows = tile_rows(step, backward)
        z = (zb_ref if backward else zf_ref)[0, rows, :]
        hq = q_ref[0, rows, :]
        q = hq * _sigmoid(hq)
        e = jnp.exp(-jnp.abs(z))
        r = 1.0 / (1.0 + e)
        er = e * r
        nonneg = z >= 0.0
        f = lb_floor + one_minus_lb * jnp.where(nonneg, r, er)
        kk = one_minus_lb * jnp.where(nonneg, er, r)
        parts = _split3(jnp.log(f))
        b2 = exact_sum(tri[1] if backward else tri[0], parts) * LOG2E
        b2_tot = exact_sum(ones_chunk, parts) * LOG2E
        qt_s[slot] = q
        b_s[slot] = b2
        c_s[slot] = b2 - jnp.log(kk) * LOG2E
        v_s[slot] = i_ref[0, rows, :]
        qe_s[slot] = (q * jnp.exp2(b2)).astype(BF16)
        ke_s[slot] = (kk * jnp.exp2(b2_tot - b2)).astype(BF16)
        dec_s[slot] = jnp.exp2(b2_tot)

    def advance(backward, slot, state):
        slabs = slabs_of[backward]
        per_chunk = len(slabs) // chunks
        attns = []
        for c in range(chunks):
            base = c * per_chunk
            for i in range(base, base + per_chunk, 2):
                ws = []
                for cc, s, half in slabs[i:i + 2]:
                    lo = cc * C + half * SUBLANES
                    c_src = jnp.broadcast_to(c_s[slot, cc * C + s:cc * C + s + 1, :], (SUBLANES, LANES))
                    expo = b_s[slot, lo:lo + SUBLANES, :] - c_src
                    if half == s // SUBLANES:
                        expo = expo + mask_s[int(backward), s % SUBLANES]
                    ws.append(qt_s[slot, lo:lo + SUBLANES, :] * jnp.exp2(expo))
                w_s[slot, i * SUBLANES:(i + 2) * SUBLANES, :] = jnp.concatenate(ws, axis=0).astype(BF16)
            attns.append(_dot(w_s[slot, base * SUBLANES:(base + per_chunk) * SUBLANES, :], hsum))

        order = range(chunks - 1, -1, -1) if backward else range(chunks)
        upds = {c: _dot(v_s[slot, c * C:(c + 1) * C, :].astype(BF16), ke_s[slot, c * C:(c + 1) * C, :], TN_DIMS)
                for c in order}
        states = {}
        for c in order:
            states[c] = state
            state = state * dec_s[slot, c * C:c * C + 1, :] + jnp.where(block, upds[c], 0.0)
        outs = {c: _dot(qe_s[slot, c * C:(c + 1) * C, :], states[c].astype(BF16), NT_DIMS) for c in order}

        pieces = []
        for c in range(chunks):
            acc = [None] * (C // SUBLANES)
            for i in range(per_chunk):
                cc, s, half = slabs[c * per_chunk + i]
                v_src = jnp.broadcast_to(v_s[slot, cc * C + s:cc * C + s + 1, :], (SUBLANES, LANES))
                term = attns[c][i * SUBLANES:(i + 1) * SUBLANES] * v_src
                acc[half] = term if acc[half] is None else acc[half] + term
            pieces.extend(acc[h] + outs[c][h * SUBLANES:(h + 1) * SUBLANES] for h in range(C // SUBLANES))
        return jnp.concatenate(pieces, axis=0), state

    for backward in (False, True):
        def trip(i, state, backward=backward):
            base = HGRN_TRIP_TILES * i
            outs = []
            for first in range(0, HGRN_TRIP_TILES, 4):
                for slot in (0, 1):
                    out, state = advance(backward, slot, state)
                    outs.append(out)
                prepare(base + first + 2, backward, 2)
                prepare(base + first + 3, backward, 3)
                for slot in (2, 3):
                    out, state = advance(backward, slot, state)
                    outs.append(out)
                prepare(jnp.minimum(base + first + 4, n_tiles - 1), backward, 0)
                prepare(jnp.minimum(base + first + 5, n_tiles - 1), backward, 1)
            for k, out in enumerate(outs):
                rows = tile_rows(base + k, backward)
                o_ref[0, rows, :] = (o_ref[0, rows, :] + out) if backward else out
            return state

        prepare(0, backward, 0)
        prepare(1, backward, 1)
        lax.fori_loop(0, n_tiles // HGRN_TRIP_TILES, trip, jnp.zeros((LANES, LANES), F32))


def _hgrn(proj, lb_logits, layer):
    B, _, S, _ = proj.shape
    n_pairs = HGRN_WIDTH // LANES

    def col(blk):
        return pl.BlockSpec((None, 1, S, LANES), lambda b, p: (b, blk + p, 0, 0))

    return pl.pallas_call(
        functools.partial(_hgrn_kernel, layer),
        out_shape=jax.ShapeDtypeStruct((B, HGRN_WIDTH // LANES, S, LANES), F32),
        grid=(B, n_pairs),
        in_specs=[col(HG_Q_BLK), col(HG_ZF_BLK), col(HG_ZB_BLK), col(HG_I_BLK),
                  pl.BlockSpec((DEPTH, LANES), lambda b, p: (0, p))],
        out_specs=pl.BlockSpec((None, 1, S, LANES), lambda b, p: (b, p, 0, 0)),
        scratch_shapes=(
            [pltpu.VMEM((HGRN_SLOTS, HGRN_TILE, LANES), F32)] * 5
            + [pltpu.VMEM((HGRN_SLOTS, HGRN_TILE, LANES), BF16)] * 2
            + [pltpu.VMEM((HGRN_SLOTS, HGRN_SLABS * SUBLANES, LANES), BF16)]
            + [pltpu.VMEM((2, SUBLANES, SUBLANES, LANES), F32)]),
        compiler_params=pltpu.CompilerParams(
            dimension_semantics=("arbitrary", "arbitrary"), vmem_limit_bytes=VMEM_LIMIT),
        name="hgrn2",
    )(proj, proj, proj, proj, lb_logits)


def _ffn_kernel(final_norm, att_ref, ret_ref, hg_ref, hga_ref, hgb_ref, x_ref, wo_ref, hgn_ref,
                gate1_ref, gffn_ref, sc2_ref, sh2_ref, gate2_ref, wgu_ref, wd_ref, gfin_ref,
                o_ref, x2_s, h_s, acc_s):
    def wide(ref):
        return jnp.concatenate([ref[0, j] for j in range(ref.shape[1])], axis=-1)

    hraw = wide(hg_ref)
    hgate = jnp.concatenate([hga_ref[0], hgb_ref[0]], axis=-1)
    hgo = (hraw * lax.rsqrt(jnp.mean(hraw * hraw, axis=-1, keepdims=True) + NORM_EPS)
           * hgn_ref[...] * (hgate * _sigmoid(hgate)))
    mixed = jnp.concatenate([wide(att_ref).astype(BF16), wide(ret_ref).astype(BF16), hgo.astype(BF16)], axis=-1)
    y = _dot(mixed, wo_ref[...])
    x2 = x_ref[0] + gate1_ref[0] * y
    x2_s[...] = x2
    n = x2 * lax.rsqrt(jnp.mean(x2 * x2, axis=-1, keepdims=True) + NORM_EPS) * gffn_ref[...]
    h_s[...] = (n * (1.0 + sc2_ref[0]) + sh2_ref[0]).astype(BF16)

    lo = 0
    for width in FFN_CHUNKS:
        h = h_s[...]
        g = _dot(h, wgu_ref[:, lo:lo + width])
        u = _dot(h, wgu_ref[:, FFN_HIDDEN + lo:FFN_HIDDEN + lo + width])
        a = (g * _sigmoid(g) * u).astype(BF16)
        down = _dot(a, wd_ref[lo:lo + width, :])
        acc_s[...] = down if lo == 0 else acc_s[...] + down
        lo += width

    out = x2_s[...] + gate2_ref[0] * acc_s[...]
    if final_norm:
        out = out * lax.rsqrt(jnp.mean(out * out, axis=-1, keepdims=True) + NORM_EPS) * gfin_ref[...]
    o_ref[0] = out


def _out_proj_ffn(att, ret, hg_raw, proj, x, wo, hgrn_gn, gate1, g_ffn, scale2, shift2, gate2,
                  w_gate_up, w_down, g_final, final_norm):
    B, S, D = x.shape
    tm = 512

    def rows(width):
        return pl.BlockSpec((1, tm, width), lambda b, i: (b, i, 0))

    def blocks(width):
        return pl.BlockSpec((1, width // LANES, tm, LANES), lambda b, i: (b, 0, i, 0))

    def per_batch():
        return pl.BlockSpec((1, 1, D), lambda b, i: (b, 0, 0))

    def vec(width):
        return pl.BlockSpec((1, width), lambda b, i: (0, 0))

    def resident(shape):
        return pl.BlockSpec(shape, lambda b, i: (0, 0), pipeline_mode=pl.Buffered(1))

    return pl.pallas_call(
        functools.partial(_ffn_kernel, final_norm),
        out_shape=jax.ShapeDtypeStruct((B, S, D), F32),
        grid=(B, S // tm),
        in_specs=[
            blocks(ATT_WIDTH), blocks(RET_WIDTH), blocks(HGRN_WIDTH),
            pl.BlockSpec((None, 1, tm, LANES), lambda b, i: (b, HG_G_BLK, i, 0)),
            pl.BlockSpec((None, 1, tm, LANES), lambda b, i: (b, HG_G_BLK + 1, i, 0)),
            rows(D),
            resident((D, D)),
            vec(HGRN_WIDTH),
            per_batch(), vec(D), per_batch(), per_batch(), per_batch(),
            resident((D, 2 * FFN_HIDDEN)),
            resident((FFN_HIDDEN, D)),
            vec(D),
        ],
        out_specs=pl.BlockSpec((1, tm, D), lambda b, i: (b, i, 0)),
        scratch_shapes=[
            pltpu.VMEM((tm, D), F32),
            pltpu.VMEM((tm, D), BF16),
            pltpu.VMEM((tm, D), F32),
        ],
        compiler_params=pltpu.CompilerParams(
            dimension_semantics=("arbitrary", "arbitrary"), vmem_limit_bytes=VMEM_LIMIT),
        name="out_proj_ffn",
    )(att, ret, hg_raw, proj, proj, x, wo, hgrn_gn.reshape(1, HGRN_WIDTH), gate1,
      g_ffn.reshape(1, D), scale2, shift2, gate2, w_gate_up, w_down, g_final.reshape(1, D))


def kernel(x, c, w_ada, b_ada, g_mix, w_in, ret_gn, hgrn_gn, hgrn_lb_logits, w_out,
           g_ffn, w_gate_up, w_down, g_final):
    B, S, D = x.shape
    mod = _modulation(c, w_ada, b_ada).reshape(DEPTH, B, N_MOD, 1, D)
    for layer in range(DEPTH):
        shift1, scale1, gate1, shift2, scale2, gate2 = (mod[layer, :, i] for i in range(N_MOD))
        proj = _in_projection(x, g_mix[layer], scale1, shift1, w_in[layer].astype(BF16))
        att = _attention(proj)
        ret = _retention(proj, ret_gn[layer])
        hg_raw = _hgrn(proj, hgrn_lb_logits, layer)
        x = _out_proj_ffn(att, ret, hg_raw, proj, x, w_out[layer].astype(BF16), hgrn_gn[layer],
                          gate1, g_ffn[layer], scale2, shift2, gate2,
                          w_gate_up[layer].astype(BF16), w_down[layer].astype(BF16), g_final,
                          final_norm=(layer == DEPTH - 1))
    return x
```

```python
import functools

import jax
import jax.numpy as jnp
from jax import lax
from jax.experimental import pallas as pl
from jax.experimental.pallas import tpu as pltpu

D_MODEL = 1024
DEPTH = 2
HEAD_DIM = 64
ATT_WIDTH = 384
RET_WIDTH = 384
HGRN_WIDTH = 256
IN_WIDTH = 3 * ATT_WIDTH + 4 * RET_WIDTH + 5 * HGRN_WIDTH
DILATED_CONFIGS = ((128, 1), (512, 4), (2048, 16))
N_SIDE = 64
RET_CHUNK = 128
HGRN_CHUNK = 16
FFN_HIDDEN = 2816
FFN_CHUNKS = (512, 512, 512, 512, 512, 256)
N_MOD = 6
NORM_EPS = 1e-6
NEG_INF = -1e30
LB_FLOOR = 1e-30

LANES = 128
SUBLANES = 8
LOG2E = 1.4426950408889634
ATT_Q_BLK, ATT_K_BLK, ATT_V_BLK = 0, 3, 6
RET_Q_BLK, RET_K_BLK, RET_V_BLK, RET_G_BLK = 9, 12, 15, 18
HG_Q_BLK, HG_ZF_BLK, HG_ZB_BLK, HG_I_BLK, HG_G_BLK = 21, 23, 25, 27, 29

VMEM_LIMIT = 56 * 1024 * 1024

F32 = jnp.float32
BF16 = jnp.bfloat16

NT_DIMS = (((1,), (1,)), ((), ()))
TN_DIMS = (((0,), (0,)), ((), ()))


def _dot(a, b, dims=None):
    if dims is None:
        return jnp.dot(a, b, preferred_element_type=F32)
    return lax.dot_general(a, b, dims, preferred_element_type=F32)


def _split2(x):
    hi = x.astype(BF16)
    lo = (x - hi.astype(F32)).astype(BF16)
    return hi, lo


def _split3(x):
    hi = x.astype(BF16)
    r1 = x - hi.astype(F32)
    mid = r1.astype(BF16)
    lo = (r1 - mid.astype(F32)).astype(BF16)
    return hi, mid, lo


def _head_block_mask(shape):
    r = lax.broadcasted_iota(jnp.int32, shape, 0)
    c = lax.broadcasted_iota(jnp.int32, shape, 1)
    return (r >= HEAD_DIM) == (c >= HEAD_DIM)


def _head_sum_matrix():
    return jnp.where(_head_block_mask((LANES, LANES)), 1.0, 0.0).astype(BF16)


def _sigmoid(x):
    return 1.0 / (1.0 + jnp.exp(-x))


def _mod_kernel(c_ref, w_ref, b_ref, o_ref):
    c = c_ref[...]
    cond = c * _sigmoid(c)
    o_ref[0] = jnp.dot(cond, w_ref[0], preferred_element_type=F32,
                       precision=lax.Precision.HIGHEST) + b_ref[0]


def _modulation(c, w_ada, b_ada):
    B, D = c.shape
    n_out = w_ada.shape[-1]
    tn = 1536
    return pl.pallas_call(
        _mod_kernel,
        out_shape=jax.ShapeDtypeStruct((DEPTH, B, n_out), F32),
        grid=(DEPTH, n_out // tn),
        in_specs=[
            pl.BlockSpec((B, D), lambda l, j: (0, 0)),
            pl.BlockSpec((1, D, tn), lambda l, j: (l, 0, j)),
            pl.BlockSpec((1, 1, tn), lambda l, j: (l, 0, j)),
        ],
        out_specs=pl.BlockSpec((1, B, tn), lambda l, j: (l, 0, j)),
        compiler_params=pltpu.CompilerParams(
            dimension_semantics=("arbitrary", "arbitrary"), vmem_limit_bytes=VMEM_LIMIT),
        name="adaln_mod",
    )(c, w_ada, b_ada.reshape(DEPTH, 1, n_out))


def _inproj_kernel(x_ref, g_ref, sc_ref, sh_ref, w_ref, o_ref):
    x = x_ref[0]
    y = x * lax.rsqrt(jnp.mean(x * x, axis=-1, keepdims=True) + NORM_EPS) * g_ref[...]
    h = y * (1.0 + sc_ref[0]) + sh_ref[0]
    res = _dot(h.astype(BF16), w_ref[...])
    for j in range(o_ref.shape[1]):
        o_ref[0, j] = res[:, j * LANES:(j + 1) * LANES]


def _in_projection(x, gain, scale, shift, w_bf16, layer):
    B, S, D = x.shape
    N = w_bf16.shape[-1]
    tm = 512
    return pl.pallas_call(
        _inproj_kernel,
        out_shape=jax.ShapeDtypeStruct((B, N // LANES, S, LANES), F32),
        grid=(B, S // tm),
        in_specs=[
            pl.BlockSpec((1, tm, D), lambda b, i: (b, i, 0)),
            pl.BlockSpec((1, D), lambda b, i: (0, 0)),
            pl.BlockSpec((1, 1, D), lambda b, i: (b, 0, 0)),
            pl.BlockSpec((1, 1, D), lambda b, i: (b, 0, 0)),
            pl.BlockSpec((None, D, N), lambda b, i: (layer, 0, 0), pipeline_mode=pl.Buffered(1)),
        ],
        out_specs=pl.BlockSpec((1, N // LANES, tm, LANES), lambda b, i: (b, 0, i, 0)),
        compiler_params=pltpu.CompilerParams(
            dimension_semantics=("arbitrary", "arbitrary"), vmem_limit_bytes=VMEM_LIMIT),
        name="in_proj",
    )(x, gain.reshape(1, D), scale, shift, w_bf16)


ATT_KEYS = 3 * N_SIDE
ATT_UNROLL = 8
RET_UNROLL = 8


def _att_kernel(q_ref, k_ref, v_ref, o_ref, bias_ref, qd_ref, kd_ref, vd_ref, t4_ref, ob_ref, mb_ref, lb_ref,
                s_ref, p_ref, m_ref):
    S = q_ref.shape[1]
    pair = pl.program_id(0)
    lane = lax.broadcasted_iota(jnp.int32, (1, LANES), 1)
    in_head = (lane < HEAD_DIM, lane >= HEAD_DIM)
    lane_q = lax.broadcasted_iota(jnp.int32, (N_SIDE, LANES), 1)
    q_in_head = (lane_q < HEAD_DIM, lane_q >= HEAD_DIM)

    @pl.when(pl.program_id(1) == 0)
    def _():
        qi = lax.broadcasted_iota(jnp.int32, (N_SIDE, ATT_KEYS), 0)
        kj = lax.broadcasted_iota(jnp.int32, (N_SIDE, ATT_KEYS), 1)
        for br, (_, dil) in enumerate(DILATED_CONFIGS):
            for case in range(3):
                steps = jnp.abs(qi + N_SIDE * case - kj)
                dist = (dil * steps).astype(F32)
                for h in range(2):
                    head = (2 * pair + h + 1).astype(F32) + jnp.zeros((1, ATT_KEYS), F32)
                    slope = jnp.exp2(-8.0 * head / (ATT_WIDTH // HEAD_DIM)) * LOG2E
                    bias_ref[br, case, h * N_SIDE:(h + 1) * N_SIDE, :] = jnp.where(
                        steps <= N_SIDE, -slope * dist, NEG_INF)

    assert [d for _, d in DILATED_CONFIGS] == [1, 4, 16]
    piece = 256
    wide = 4
    quarter = S // 4
    assert S // 16 == piece and (S // piece) % wide == 0
    for src_ref, dst_ref, scale in ((q_ref, qd_ref, HEAD_DIM ** -0.5 * LOG2E), (k_ref, kd_ref, 1.0),
                                    (v_ref, vd_ref, 1.0)):
        def pass1(t, carry, src_ref=src_ref, dst_ref=dst_ref, scale=scale):
            dsts, plain, strided = [], [], []
            for w in range(wide):
                j = t * wide + w
                dst = pl.ds(pl.multiple_of(j * piece, piece), piece)
                cls = j // (quarter // piece)
                first = (j % (quarter // piece)) * piece
                dsts.append(dst)
                plain.append(src_ref[0, dst, :] * scale)
                strided.append(src_ref[0, pl.ds(cls + 4 * first, piece, stride=4), :] * scale)
            for dst, x0, x4 in zip(dsts, plain, strided):
                dst_ref[0, dst, :] = x0.astype(BF16)
                t4_ref[dst, :] = x4
                dst_ref[1, dst, :] = x4.astype(BF16)
            return carry

        lax.fori_loop(0, S // piece // wide, pass1, 0)

        def pass2(t, carry, dst_ref=dst_ref):
            xs = []
            for w in range(wide):
                j = t * wide + w
                xs.append(t4_ref[pl.ds((j % 4) * quarter + j // 4, piece, stride=4), :])
            for w, x in enumerate(xs):
                dst = pl.ds(pl.multiple_of((t * wide + w) * piece, piece), piece)
                dst_ref[2, dst, :] = x.astype(BF16)
            return carry

        lax.fori_loop(0, S // piece // wide, pass2, 0)

    zero_q = jnp.zeros((N_SIDE, LANES), BF16)
    ones_v = jnp.ones((ATT_KEYS, LANES), BF16)
    branch_stages = []
    for br, (_, dil) in enumerate(DILATED_CONFIGS):
        sub_len = S // dil
        n_blocks = sub_len // N_SIDE
        blk_bits = n_blocks.bit_length() - 1

        n_groups = dil * n_blocks // ATT_UNROLL

        def block_index(g, u, n_blocks=n_blocks, blk_bits=blk_bits, sub_len=sub_len):
            idx = g * ATT_UNROLL + u
            c = jnp.right_shift(idx, blk_bits)
            n = jnp.bitwise_and(idx, n_blocks - 1)
            win = jnp.clip(N_SIDE * n - N_SIDE, 0, sub_len - ATT_KEYS)
            case = jnp.right_shift(N_SIDE * n - win, 6)
            q_src = pl.ds(pl.multiple_of(N_SIDE * idx, N_SIDE), N_SIDE)
            k_src = pl.ds(pl.multiple_of(N_SIDE * (idx - n) + win, N_SIDE), ATT_KEYS)
            return c, n, case, q_src, k_src

        def scores_stage(g, slot, br=br, block_index=block_index):
            for u in range(ATT_UNROLL):
                _, _, case, q_src, k_src = block_index(g, u)
                q = qd_ref[br, q_src, :]
                q2 = jnp.concatenate([jnp.where(q_in_head[h], q, zero_q) for h in range(2)], axis=0)
                s_ref[slot, u] = _dot(q2, kd_ref[br, k_src, :], NT_DIMS) + bias_ref[br, case]

        def values_stage(g, slot, br=br, dil=dil, block_index=block_index):
            for u in range(ATT_UNROLL):
                c, n, _, q_src, k_src = block_index(g, u)
                rows = q_src if dil == 1 else pl.ds(c + dil * N_SIDE * n, N_SIDE, stride=dil)
                ol = _dot(p_ref[slot, u], jnp.concatenate([vd_ref[br, k_src, :], ones_v], axis=1))
                o, l = ol[:, :LANES], ol[:, LANES:]
                m = m_ref[slot, u]
                ob_ref[br, rows, :] = jnp.where(in_head[0], o[:N_SIDE], o[N_SIDE:])
                mb_ref[br, rows, :] = jnp.where(in_head[0], m[:N_SIDE], m[N_SIDE:])
                lb_ref[br, rows, :] = jnp.where(in_head[0], l[:N_SIDE], l[N_SIDE:])

        branch_stages.append((scores_stage, values_stage, n_groups))

    def softmax_stage(slot):
        for u in range(ATT_UNROLL):
            s = s_ref[slot, u]
            m = jnp.max(s, axis=-1, keepdims=True)
            p = jnp.exp2(s - m)
            p_ref[slot, u] = p.astype(BF16)
            m_ref[slot, u] = jnp.broadcast_to(m, (2 * N_SIDE, LANES))

    def tick(stage1, stage2_slot, stage3):
        if stage3 is not None:
            stage3[0](stage3[1], stage3[2])
        if stage2_slot is not None:
            softmax_stage(stage2_slot)
        if stage1 is not None:
            stage1[0](stage1[1], stage1[2])

    prev = None
    for scores_stage, values_stage, n_groups in branch_stages:
        assert n_groups % 2 == 0
        for par in range(2):
            older = None if prev is None else (prev[0], prev[1] - 2 + par, par)
            middle = (1 - par) if (prev is not None or par == 1) else None
            tick((scores_stage, par, par), middle, older)

        def trip(i, carry, scores_stage=scores_stage, values_stage=values_stage):
            for par in range(2):
                g = 2 * i + par
                tick((scores_stage, g, par), 1 - par, (values_stage, g - 2, par))
            return carry

        lax.fori_loop(1, n_groups // 2, trip, 0)
        prev = (values_stage, n_groups)
    tick(None, 1, (prev[0], prev[1] - 2, 0))
    tick(None, None, (prev[0], prev[1] - 1, 1))

    tile = 512

    def merge_body(t, carry):
        rows = pl.ds(pl.multiple_of(t * tile, tile), tile)
        ms = [mb_ref[b, rows, :] for b in range(len(DILATED_CONFIGS))]
        top = functools.reduce(jnp.maximum, ms)
        es = [jnp.exp2(m - top) for m in ms]
        num = functools.reduce(lambda a, b: a + b, [e * ob_ref[b, rows, :] for b, e in enumerate(es)])
        den = functools.reduce(lambda a, b: a + b, [e * lb_ref[b, rows, :] for b, e in enumerate(es)])
        o_ref[0, rows, :] = num / den
        return carry

    lax.fori_loop(0, S // tile, merge_body, 0)


def _attention(proj):
    B, _, S, _ = proj.shape
    n_pairs = ATT_WIDTH // LANES
    n_br = len(DILATED_CONFIGS)

    def col(blk):
        return pl.BlockSpec((None, 1, S, LANES), lambda p, b: (b, blk + p, 0, 0))

    return pl.pallas_call(
        _att_kernel,
        out_shape=jax.ShapeDtypeStruct((B, ATT_WIDTH // LANES, S, LANES), F32),
        grid=(n_pairs, B),
        in_specs=[col(ATT_Q_BLK), col(ATT_K_BLK), col(ATT_V_BLK)],
        out_specs=pl.BlockSpec((None, 1, S, LANES), lambda p, b: (b, p, 0, 0)),
        scratch_shapes=[
            pltpu.VMEM((n_br, 3, 2 * N_SIDE, ATT_KEYS), F32),
            pltpu.VMEM((n_br, S, LANES), BF16),
            pltpu.VMEM((n_br, S, LANES), BF16),
            pltpu.VMEM((n_br, S, LANES), BF16),
            pltpu.VMEM((S, LANES), F32),
            pltpu.VMEM((n_br, S, LANES), F32),
            pltpu.VMEM((n_br, S, LANES), F32),
            pltpu.VMEM((n_br, S, LANES), F32),
            pltpu.VMEM((2, ATT_UNROLL, 2 * N_SIDE, ATT_KEYS), F32),
            pltpu.VMEM((2, ATT_UNROLL, 2 * N_SIDE, ATT_KEYS), BF16),
            pltpu.VMEM((2, ATT_UNROLL, 2 * N_SIDE, LANES), F32),
        ],
        compiler_params=pltpu.CompilerParams(
            dimension_semantics=("arbitrary", "arbitrary"), vmem_limit_bytes=VMEM_LIMIT),
        name="dilated_attention",
    )(proj, proj, proj)


def _ret_kernel(q_ref, k_ref, v_ref, g_ref, gn_ref, o_ref, sf_ref, sb_ref):
    S = q_ref.shape[1]
    C = RET_CHUNK
    n_chunks = S // C
    pair = pl.program_id(1)
    lane = lax.broadcasted_iota(jnp.int32, (1, LANES), 1)
    in_head = (lane < HEAD_DIM, lane >= HEAD_DIM)
    block = _head_block_mask((LANES, LANES))
    hsum = _head_sum_matrix()

    def log_gamma(head_f32):
        return jnp.log1p(-jnp.exp2(-5.0 - head_f32))

    zeros_row = jnp.zeros((1, LANES), F32)
    lg_head = [log_gamma((2 * pair + h).astype(F32) + zeros_row) for h in range(2)]
    lg = jnp.where(in_head[0], lg_head[0], lg_head[1])
    pos = lax.broadcasted_iota(jnp.int32, (C, LANES), 0).astype(F32)
    k_fwd = jnp.exp(lg * (C - 1.0 - pos))
    k_bwd = jnp.exp(lg * pos)
    q_fwd = jnp.exp(lg * (pos + 1.0))
    q_bwd = jnp.exp(lg * (C - pos))
    state_decay = jnp.where(block, jnp.exp(lg * C), 0.0)
    ri = lax.broadcasted_iota(jnp.int32, (C, C), 0)
    ci = lax.broadcasted_iota(jnp.int32, (C, C), 1)
    dist = jnp.abs(ri - ci).astype(F32)
    intra = [jnp.exp(lg_head[h] * dist) for h in range(2)]

    def chunk_rows(n):
        return pl.ds(pl.multiple_of(n * C, C), C)

    def incr_body(it, carry):
        for u in range(RET_UNROLL):
            n = it * RET_UNROLL + u
            rows = chunk_rows(n)
            k = k_ref[0, rows, :] * (HEAD_DIM ** -0.5)
            v = v_ref[0, rows, :]
            sf_ref[n] = jnp.where(block, _dot(k * k_fwd, v, TN_DIMS), 0.0)
            sb_ref[n] = jnp.where(block, _dot(k * k_bwd, v, TN_DIMS), 0.0)
        return carry

    lax.fori_loop(0, n_chunks // RET_UNROLL, incr_body, 0)

    def scan_fwd(n, state):
        incr = sf_ref[n]
        sf_ref[n] = state
        return state_decay * state + incr

    def scan_bwd(i, state):
        n = n_chunks - 1 - i
        incr = sb_ref[n]
        sb_ref[n] = state
        return state_decay * state + incr

    lax.fori_loop(0, n_chunks, scan_fwd, jnp.zeros((LANES, LANES), F32), unroll=4)
    lax.fori_loop(0, n_chunks, scan_bwd, jnp.zeros((LANES, LANES), F32), unroll=4)

    gn = gn_ref[...]

    def out_body(it, carry):
        rows = [chunk_rows(it * RET_UNROLL + u) for u in range(RET_UNROLL)]
        qs = [q_ref[0, r, :] for r in rows]
        vs = [v_ref[0, r, :] for r in rows]
        scores = []
        for u in range(RET_UNROLL):
            k = k_ref[0, rows[u], :] * (HEAD_DIM ** -0.5)
            scores.append([_dot(jnp.where(in_head[h], qs[u], 0.0), k, NT_DIMS) for h in range(2)])
        outs = []
        for u in range(RET_UNROLL):
            n = it * RET_UNROLL + u
            lhs = jnp.concatenate([(qs[u] * q_fwd).astype(BF16), (qs[u] * q_bwd).astype(BF16)], axis=1)
            rhs = jnp.concatenate([sf_ref[n].astype(BF16), sb_ref[n].astype(BF16)], axis=0)
            outs.append(_dot(lhs, rhs))
        for u in range(RET_UNROLL):
            lhs = jnp.concatenate([(scores[u][h] * intra[h]).astype(BF16) for h in range(2)], axis=1)
            rhs = jnp.concatenate([jnp.where(in_head[h], vs[u], 0.0).astype(BF16) for h in range(2)], axis=0)
            outs[u] = outs[u] + _dot(lhs, rhs)
        hsum2 = jnp.concatenate([hsum, hsum], axis=0)
        mus = []
        for u in range(RET_UNROLL):
            mus.append(_dot(jnp.concatenate(_split2(outs[u]), axis=1), hsum2) * (1.0 / HEAD_DIM))
        cens = [outs[u] - mus[u] for u in range(RET_UNROLL)]
        for u in range(RET_UNROLL):
            var = _dot(jnp.concatenate(_split2(cens[u] * cens[u]), axis=1), hsum2) * (1.0 / HEAD_DIM)
            g = g_ref[0, rows[u], :]
            o_ref[0, rows[u], :] = cens[u] * lax.rsqrt(var + NORM_EPS) * gn * (g * _sigmoid(g))
        return carry

    lax.fori_loop(0, n_chunks // RET_UNROLL, out_body, 0)


def _retention(proj, ret_gn):
    B, _, S, _ = proj.shape
    n_pairs = RET_WIDTH // LANES

    def col(blk):
        return pl.BlockSpec((None, 1, S, LANES), lambda b, p: (b, blk + p, 0, 0))

    return pl.pallas_call(
        _ret_kernel,
        out_shape=jax.ShapeDtypeStruct((B, RET_WIDTH // LANES, S, LANES), F32),
        grid=(B, n_pairs),
        in_specs=[col(RET_Q_BLK), col(RET_K_BLK), col(RET_V_BLK), col(RET_G_BLK),
                  pl.BlockSpec((1, LANES), lambda b, p: (0, p))],
        out_specs=pl.BlockSpec((None, 1, S, LANES), lambda b, p: (b, p, 0, 0)),
        scratch_shapes=[pltpu.VMEM((S // RET_CHUNK, LANES, LANES), F32),
                        pltpu.VMEM((S // RET_CHUNK, LANES, LANES), F32)],
        compiler_params=pltpu.CompilerParams(
            dimension_semantics=("arbitrary", "arbitrary"), vmem_limit_bytes=VMEM_LIMIT),
        name="retention",
    )(proj, proj, proj, proj, ret_gn.reshape(1, RET_WIDTH))


HGRN_TILE = 128
HGRN_SLABS = HGRN_TILE * (HGRN_CHUNK // SUBLANES) * 3 // 4
HGRN_SLOTS = 4
HGRN_TRIP_TILES = 8


def _hgrn_kernel(layer, q_ref, zf_ref, zb_ref, i_ref, lbl_ref, o_ref,
                 qt_s, b_s, c_s, v_s, dec_s, qe_s, ke_s, w_s, mask_s):
    S = q_ref.shape[1]
    T, C = HGRN_TILE, HGRN_CHUNK
    n_tiles = S // T
    chunks = T // C
    block = _head_block_mask((LANES, LANES))
    hsum = _head_sum_matrix()

    logits = [lbl_ref[l:l + 1, :] for l in range(DEPTH)]
    mx = functools.reduce(jnp.maximum, logits)
    ex = [jnp.exp(t - mx) for t in logits]
    den = functools.reduce(lambda a, b: a + b, ex)
    probs = [e / den for e in ex]
    cum = functools.reduce(lambda a, b: a + b, probs[:layer + 1])
    lb = jnp.clip(cum - probs[0], 0.0, 1.0 - 1e-6)
    lb_floor = jnp.maximum(lb, LB_FLOOR)
    one_minus_lb = 1.0 - lb

    ri = lax.broadcasted_iota(jnp.int32, (T, T), 0)
    ci = lax.broadcasted_iota(jnp.int32, (T, T), 1)
    chunk_bits = C.bit_length() - 1
    same_chunk = jnp.right_shift(ri, chunk_bits) == jnp.right_shift(ci, chunk_bits)
    tri = (jnp.where(same_chunk & (ci <= ri), 1.0, 0.0).astype(BF16),
           jnp.where(same_chunk & (ci >= ri), 1.0, 0.0).astype(BF16))
    ones_chunk = jnp.where(same_chunk, 1.0, 0.0).astype(BF16)

    def exact_sum(mat, parts):
        return _dot(jnp.concatenate([mat] * len(parts), axis=1), jnp.concatenate(parts, axis=0))

    t_loc = lax.broadcasted_iota(jnp.int32, (SUBLANES, LANES), 0)
    for s in range(SUBLANES):
        mask_s[0, s] = jnp.where(t_loc >= s, 0.0, NEG_INF)
        mask_s[1, s] = jnp.where(t_loc <= s, 0.0, NEG_INF)

    def slab_list(backward):
        slabs = []
        for c in range(chunks):
            for s in range(C):
                for half in range(C // SUBLANES):
                    if (half <= s // SUBLANES) if backward else (half >= s // SUBLANES):
                        slabs.append((c, s, half))
        return slabs

    slabs_of = {False: slab_list(False), True: slab_list(True)}

    def tile_rows(step, backward):
        tile = (n_tiles - 1 - step) if backward else step
        if isinstance(tile, int):
            return pl.ds(tile * T, T)
        return pl.ds(pl.multiple_of(tile * T, T), T)

    def prepare(step, backward, slot):
        rows = tile_rows(step, backward)
        z = (zb_ref if backward else zf_ref)[0, rows, :]
        hq = q_ref[0, rows, :]
        q = hq * _sigmoid(hq)
        e = jnp.exp(-jnp.abs(z))
        r = 1.0 / (1.0 + e)
        er = e * r
        nonneg = z >= 0.0
        f = lb_floor + one_minus_lb * jnp.where(nonneg, r, er)
        kk = one_minus_lb * jnp.where(nonneg, er, r)
        parts = _split3(jnp.log(f))
        b2 = exact_sum(tri[1] if backward else tri[0], parts) * LOG2E
        b2_tot = exact_sum(ones_chunk, parts) * LOG2E
        qt_s[slot] = q
        b_s[slot] = b2
        c_s[slot] = b2 - jnp.log(kk) * LOG2E
        v_s[slot] = i_ref[0, rows, :]
        qe_s[slot] = (q * jnp.exp2(b2)).astype(BF16)
        ke_s[slot] = (kk * jnp.exp2(b2_tot - b2)).astype(BF16)
        dec_s[slot] = jnp.exp2(b2_tot)

    def advance(backward, slot, state):
        slabs = slabs_of[backward]
        per_chunk = len(slabs) // chunks
        attns = []
        for c in range(chunks):
            base = c * per_chunk
            for i in range(base, base + per_chunk, 2):
                ws = []
                for cc, s, half in slabs[i:i + 2]:
                    lo = cc * C + half * SUBLANES
                    c_src = jnp.broadcast_to(c_s[slot, cc * C + s:cc * C + s + 1, :], (SUBLANES, LANES))
                    expo = b_s[slot, lo:lo + SUBLANES, :] - c_src
                    if half == s // SUBLANES:
                        expo = expo + mask_s[int(backward), s % SUBLANES]
                    ws.append(qt_s[slot, lo:lo + SUBLANES, :] * jnp.exp2(expo))
                w_s[slot, i * SUBLANES:(i + 2) * SUBLANES, :] = jnp.concatenate(ws, axis=0).astype(BF16)
            attns.append(_dot(w_s[slot, base * SUBLANES:(base + per_chunk) * SUBLANES, :], hsum))

        order = range(chunks - 1, -1, -1) if backward else range(chunks)
        upds = {c: _dot(v_s[slot, c * C:(c + 1) * C, :].astype(BF16), ke_s[slot, c * C:(c + 1) * C, :], TN_DIMS)
                for c in order}
        states = {}
        for c in order:
            states[c] = state
            state = state * dec_s[slot, c * C:c * C + 1, :] + jnp.where(block, upds[c], 0.0)
        outs = {c: _dot(qe_s[slot, c * C:(c + 1) * C, :], states[c].astype(BF16), NT_DIMS) for c in order}

        pieces = []
        for c in range(chunks):
            acc = [None] * (C // SUBLANES)
            for i in range(per_chunk):
                cc, s, half = slabs[c * per_chunk + i]
                v_src = jnp.broadcast_to(v_s[slot, cc * C + s:cc * C + s + 1, :], (SUBLANES, LANES))
                term = attns[c][i * SUBLANES:(i + 1) * SUBLANES] * v_src
                acc[half] = term if acc[half] is None else acc[half] + term
            pieces.extend(acc[h] + outs[c][h * SUBLANES:(h + 1) * SUBLANES] for h in range(C // SUBLANES))
        return jnp.concatenate(pieces, axis=0), state

    for backward in (False, True):
        def trip(i, state, backward=backward):
            base = HGRN_TRIP_TILES * i
            outs = []
            for first in range(0, HGRN_TRIP_TILES, 4):
                for slot in (0, 1):
                    out, state = advance(backward, slot, state)
                    outs.append(out)
                prepare(base + first + 2, backward, 2)
                prepare(base + first + 3, backward, 3)
                for slot in (2, 3):
                    out, state = advance(backward, slot, state)
                    outs.append(out)
                prepare(jnp.minimum(base + first + 4, n_tiles - 1), backward, 0)
                prepare(jnp.minimum(base + first + 5, n_tiles - 1), backward, 1)
            for k, out in enumerate(outs):
                rows = tile_rows(base + k, backward)
                o_ref[0, rows, :] = (o_ref[0, rows, :] + out) if backward else out
            return state

        prepare(0, backward, 0)
        prepare(1, backward, 1)
        lax.fori_loop(0, n_tiles // HGRN_TRIP_TILES, trip, jnp.zeros((LANES, LANES), F32))


def _hgrn(proj, lb_logits, layer):
    B, _, S, _ = proj.shape
    n_pairs = HGRN_WIDTH // LANES

    def col(blk):
        return pl.BlockSpec((None, 1, S, LANES), lambda b, p: (b, blk + p, 0, 0))

    return pl.pallas_call(
        functools.partial(_hgrn_kernel, layer),
        out_shape=jax.ShapeDtypeStruct((B, HGRN_WIDTH // LANES, S, LANES), F32),
        grid=(B, n_pairs),
        in_specs=[col(HG_Q_BLK), col(HG_ZF_BLK), col(HG_ZB_BLK), col(HG_I_BLK),
                  pl.BlockSpec((DEPTH, LANES), lambda b, p: (0, p))],
        out_specs=pl.BlockSpec((None, 1, S, LANES), lambda b, p: (b, p, 0, 0)),
        scratch_shapes=(
            [pltpu.VMEM((HGRN_SLOTS, HGRN_TILE, LANES), F32)] * 5
            + [pltpu.VMEM((HGRN_SLOTS, HGRN_TILE, LANES), BF16)] * 2
            + [pltpu.VMEM((HGRN_SLOTS, HGRN_SLABS * SUBLANES, LANES), BF16)]
            + [pltpu.VMEM((2, SUBLANES, SUBLANES, LANES), F32)]),
        compiler_params=pltpu.CompilerParams(
            dimension_semantics=("arbitrary", "arbitrary"), vmem_limit_bytes=VMEM_LIMIT),
        name="hgrn2",
    )(proj, proj, proj, proj, lb_logits)


def _ffn_kernel(final_norm, att_ref, ret_ref, hg_ref, hga_ref, hgb_ref, x_ref, wo_ref, hgn_ref,
                gate1_ref, gffn_ref, sc2_ref, sh2_ref, gate2_ref, wgu_ref, wd_ref, gfin_ref,
                o_ref, x2_s, h_s, acc_s):
    def wide(ref):
        return jnp.concatenate([ref[0, j] for j in range(ref.shape[1])], axis=-1)

    hraw = wide(hg_ref)
    hgate = jnp.concatenate([hga_ref[0], hgb_ref[0]], axis=-1)
    hgo = (hraw * lax.rsqrt(jnp.mean(hraw * hraw, axis=-1, keepdims=True) + NORM_EPS)
           * hgn_ref[...] * (hgate * _sigmoid(hgate)))
    mixed = jnp.concatenate([wide(att_ref).astype(BF16), wide(ret_ref).astype(BF16), hgo.astype(BF16)], axis=-1)
    y = _dot(mixed, wo_ref[...])
    x2 = x_ref[0] + gate1_ref[0] * y
    x2_s[...] = x2
    n = x2 * lax.rsqrt(jnp.mean(x2 * x2, axis=-1, keepdims=True) + NORM_EPS) * gffn_ref[...]
    h_s[...] = (n * (1.0 + sc2_ref[0]) + sh2_ref[0]).astype(BF16)

    lo = 0
    for width in FFN_CHUNKS:
        h = h_s[...]
        g = _dot(h, wgu_ref[:, lo:lo + width])
        u = _dot(h, wgu_ref[:, FFN_HIDDEN + lo:FFN_HIDDEN + lo + width])
        a = (g * _sigmoid(g) * u).astype(BF16)
        down = _dot(a, wd_ref[lo:lo + width, :])
        acc_s[...] = down if lo == 0 else acc_s[...] + down
        lo += width

    out = x2_s[...] + gate2_ref[0] * acc_s[...]
    if final_norm:
        out = out * lax.rsqrt(jnp.mean(out * out, axis=-1, keepdims=True) + NORM_EPS) * gfin_ref[...]
    o_ref[0] = out


def _out_proj_ffn(att, ret, hg_raw, proj, x, wo, hgrn_gn, gate1, g_ffn, scale2, shift2, gate2,
                  w_gate_up, w_down, g_final, layer):
    B, S, D = x.shape
    final_norm = layer == DEPTH - 1
    tm = 512

    def rows(width):
        return pl.BlockSpec((1, tm, width), lambda b, i: (b, i, 0))

    def blocks(width):
        return pl.BlockSpec((1, width // LANES, tm, LANES), lambda b, i: (b, 0, i, 0))

    def per_batch():
        return pl.BlockSpec((1, 1, D), lambda b, i: (b, 0, 0))

    def vec(width):
        return pl.BlockSpec((1, width), lambda b, i: (0, 0))

    def resident(shape):
        return pl.BlockSpec((None,) + shape, lambda b, i: (layer, 0, 0), pipeline_mode=pl.Buffered(1))

    return pl.pallas_call(
        functools.partial(_ffn_kernel, final_norm),
        out_shape=jax.ShapeDtypeStruct((B, S, D), F32),
        grid=(B, S // tm),
        in_specs=[
            blocks(ATT_WIDTH), blocks(RET_WIDTH), blocks(HGRN_WIDTH),
            pl.BlockSpec((None, 1, tm, LANES), lambda b, i: (b, HG_G_BLK, i, 0)),
            pl.BlockSpec((None, 1, tm, LANES), lambda b, i: (b, HG_G_BLK + 1, i, 0)),
            rows(D),
            resident((D, D)),
            vec(HGRN_WIDTH),
            per_batch(), vec(D), per_batch(), per_batch(), per_batch(),
            resident((D, 2 * FFN_HIDDEN)),
            resident((FFN_HIDDEN, D)),
            vec(D),
        ],
        out_specs=pl.BlockSpec((1, tm, D), lambda b, i: (b, i, 0)),
        scratch_shapes=[
            pltpu.VMEM((tm, D), F32),
            pltpu.VMEM((tm, D), BF16),
            pltpu.VMEM((tm, D), F32),
        ],
        compiler_params=pltpu.CompilerParams(
            dimension_semantics=("arbitrary", "arbitrary"), vmem_limit_bytes=VMEM_LIMIT),
        name="out_proj_ffn",
    )(att, ret, hg_raw, proj, proj, x, wo, hgrn_gn.reshape(1, HGRN_WIDTH), gate1,
      g_ffn.reshape(1, D), scale2, shift2, gate2, w_gate_up, w_down, g_final.reshape(1, D))


def kernel(x, c, w_ada, b_ada, g_mix, w_in, ret_gn, hgrn_gn, hgrn_lb_logits, w_out,
           g_ffn, w_gate_up, w_down, g_final):
    B, S, D = x.shape
    mod = _modulation(c, w_ada, b_ada).reshape(DEPTH, B, N_MOD, 1, D)
    w_in, w_out, w_gate_up, w_down = (w.astype(BF16) for w in (w_in, w_out, w_gate_up, w_down))
    for layer in range(DEPTH):
        shift1, scale1, gate1, shift2, scale2, gate2 = (mod[layer, :, i] for i in range(N_MOD))
        proj = _in_projection(x, g_mix[layer], scale1, shift1, w_in, layer)
        att = _attention(proj)
        ret = _retention(proj, ret_gn[layer])
        hg_raw = _hgrn(proj, hgrn_lb_logits, layer)
        x = _out_proj_ffn(att, ret, hg_raw, proj, x, w_out, hgrn_gn[layer],
                          gate1, g_ffn[layer], scale2, shift2, gate2,
                          w_gate_up, w_down, g_final, layer)
    return x
```
